```python
import math
import jax, jax.numpy as jnp
from jax import lax
import numpy as np

D_MODEL = 1024
BATCH = 8
SEQ = 4096
DEPTH = 2
DEC_BATCH = 32
DEC_SEQ = 8
PAST_LEN = 16384
PAGE_SIZE = 128

D_MIX = D_MODEL
ATT_HEADS = 8
ATT_HD = 64
D_ATT = ATT_HEADS * ATT_HD
DILATED_BRANCHES = ((128, 1), (512, 4), (2048, 16))
MAX_WINDOW = 2048
REL_BUCKETS = 32
REL_MAX_DIST = 2048
CONV_CH = D_MIX // 4
CONV_WIDTH = 31
RET_HEADS = 4
RET_HDK = 32
RET_HDV = 64
D_RET_K = RET_HEADS * RET_HDK
D_RET_V = RET_HEADS * RET_HDV
RET_CHUNK = 128
IN_SIZES = (D_ATT, D_ATT, D_ATT, 2 * CONV_CH, D_RET_K, D_RET_K, D_RET_V, D_RET_V)
D_IN = 3 * D_ATT + 2 * CONV_CH + 2 * D_RET_K + 2 * D_RET_V
N_GROUPS = 4
EXPERTS_PER_GROUP = 8
N_EXPERTS = N_GROUPS * EXPERTS_PER_GROUP
TOP_K_IN_GROUP = 2
D_EXPERT = 512
MOE_BLOCK = 128
NORM_EPS = 1e-6
NEG_INF = -1e30

kernel_name = 'hymba_dilated_conformer_retnet_hmoe'


def rms_norm(x, g):
    xf = x.astype(jnp.float32)
    y = xf * lax.rsqrt(jnp.mean(xf * xf, axis=-1, keepdims=True) + NORM_EPS)
    return (y * g.astype(jnp.float32)).astype(x.dtype)


def layer_norm_f32(x, eps=1e-5):
    xf = x.astype(jnp.float32)
    mu = jnp.mean(xf, axis=-1, keepdims=True)
    var = jnp.mean(jnp.square(xf - mu), axis=-1, keepdims=True)
    return (xf - mu) * lax.rsqrt(var + eps)


def ada_modulation(c, w_ada, b_ada):
    m = jax.nn.silu(c) @ w_ada + b_ada
    return jnp.split(m[:, None, :], 6, axis=-1)


def t5_bucket(dist):
    n = np.asarray(dist)
    max_exact = REL_BUCKETS // 2
    large = max_exact + (np.log(np.maximum(n, 1) / max_exact) / math.log(REL_MAX_DIST / max_exact)
                         * (REL_BUCKETS - max_exact)).astype(np.int64)
    large = np.minimum(large, REL_BUCKETS - 1)
    return np.where(n < max_exact, n, large).astype(np.int32)


def dilated_attention_prompt(q, k, v, rel_bias_table):
    B, S, H, hd = q.shape
    scale = hd ** -0.5
    outs, lses = [], []
    for window, dil in DILATED_BRANCHES:
        steps = window // dil
        blk = steps
        span = dil * blk
        n_blk = -(-S // span)
        s_pad = n_blk * span

        def to_sub(t):
            t = jnp.pad(t, ((0, 0), (0, s_pad - S), (0, 0), (0, 0)))
            t = t.reshape(B, n_blk * blk, dil, H, t.shape[-1]).transpose(0, 2, 1, 3, 4)
            return t.reshape(B, dil, n_blk, blk, H, t.shape[-1])

        def with_prev(t):
            prev = jnp.pad(t[:, :, :-1], ((0, 0), (0, 0), (1, 0), (0, 0), (0, 0), (0, 0)))
            return jnp.concatenate([prev, t], axis=3)

        def from_sub(t):
            t = t.reshape(B, dil, n_blk * blk, H, t.shape[-1]).transpose(0, 2, 1, 3, 4)
            return t.reshape(B, s_pad, H, t.shape[-1])[:, :S]

        qs = to_sub(q)
        kk = with_prev(to_sub(k))
        vv = with_prev(to_sub(v))
        qi = np.arange(blk)[:, None]
        kj = np.arange(2 * blk)[None, :]
        delta = qi - kj + blk
        band = (delta >= 0) & (delta <= steps)
        first = band & (kj >= blk)
        mask = np.concatenate([first[None], np.broadcast_to(band, (n_blk - 1,) + band.shape)], 0)
        bucket = t5_bucket(dil * np.clip(delta, 0, steps))
        bias = jnp.transpose(rel_bias_table[bucket], (2, 0, 1)).astype(jnp.float32)
        logits = jnp.einsum('brnqhd,brnkhd->brnhqk', qs, kk).astype(jnp.float32) * scale + bias
        logits = jnp.where(mask[None, None, :, None], logits, NEG_INF)
        lse = jax.nn.logsumexp(logits, axis=-1)
        p = jnp.exp(logits - lse[..., None])
        o = jnp.einsum('brnhqk,brnkhd->brnqhd', p.astype(v.dtype), vv)
        outs.append(from_sub(o))
        lses.append(from_sub(jnp.swapaxes(lse, -1, -2)[..., None])[..., 0])
    w = jax.nn.softmax(jnp.stack(lses, 0), axis=0)
    out = sum(w[i][..., None] * outs[i].astype(jnp.float32) for i in range(len(outs)))
    return out.astype(q.dtype)


def dilated_attention_sample(q, k_new, v_new, k_buf, v_buf, rel_bias_table):
    W = k_buf.shape[1]
    T = q.shape[1]
    hd = q.shape[-1]
    kc = jnp.concatenate([k_buf.astype(k_new.dtype), k_new], axis=1)
    vc = jnp.concatenate([v_buf.astype(v_new.dtype), v_new], axis=1)
    outs, lses = [], []
    for window, dil in DILATED_BRANCHES:
        steps = window // dil
        i = np.arange(steps + 1)
        loc = W + np.arange(T)[:, None] - dil * i[None, :]
        valid = loc >= 0
        idx = np.maximum(loc, 0)
        kg = kc[:, idx]
        vg = vc[:, idx]
        bias = rel_bias_table[t5_bucket(dil * i)].T.astype(jnp.float32)
        logits = jnp.einsum('bthd,btkhd->bhtk', q, kg).astype(jnp.float32) * hd ** -0.5 + bias[None, :, None, :]
        logits = jnp.where(valid[None, None], logits, NEG_INF)
        lse = jax.nn.logsumexp(logits, axis=-1)
        p = jnp.exp(logits - lse[..., None])
        outs.append(jnp.einsum('bhtk,btkhd->bthd', p.astype(vg.dtype), vg))
        lses.append(jnp.swapaxes(lse, 1, 2))
    w = jax.nn.softmax(jnp.stack(lses, 0), axis=0)
    out = sum(w[i][..., None] * outs[i].astype(jnp.float32) for i in range(len(outs)))
    return out.astype(q.dtype)


def conformer_conv(u, conv_buf, conv_w, conv_b, ln_g, ln_b):
    a, b = jnp.split(u, 2, axis=-1)
    h = a * jax.nn.sigmoid(b)
    hc = jnp.concatenate([conv_buf.astype(h.dtype), h], axis=1)
    y = lax.conv_general_dilated(hc, conv_w[:, None, :].astype(hc.dtype), window_strides=(1,), padding='VALID',
                                 dimension_numbers=('NWC', 'WIO', 'NWC'), feature_group_count=CONV_CH) + conv_b
    y = layer_norm_f32(y) * ln_g + ln_b
    y = jax.nn.silu(y)
    return y.astype(u.dtype), hc[:, -(CONV_WIDTH - 1):]


def rotary(x, pos):
    half = x.shape[-1] // 2
    inv = 1.0 / (10000.0 ** jnp.linspace(0.0, 1.0, half, dtype=jnp.float32))
    ang = pos.astype(jnp.float32)[:, None] * inv[None, :]
    cos = jnp.cos(ang)[None, :, None, :]
    sin = jnp.sin(ang)[None, :, None, :]
    x1, x2 = x[..., :half], x[..., half:]
    return jnp.concatenate([x1 * cos - x2 * sin, x1 * sin + x2 * cos], axis=-1)


def retention(q, k, v, g, state, pos):
    B, S, _ = q.shape
    f32 = jnp.float32
    q = rotary(q.reshape(B, S, RET_HEADS, RET_HDK).astype(f32), pos) * RET_HDK ** -0.5
    k = rotary(k.reshape(B, S, RET_HEADS, RET_HDK).astype(f32), pos)
    v = v.reshape(B, S, RET_HEADS, RET_HDV).astype(f32)
    log_g = jnp.log1p(-jnp.exp2(-5.0 - jnp.arange(RET_HEADS, dtype=f32)))
    chunk = RET_CHUNK if S % RET_CHUNK == 0 else S
    n_chunks = S // chunk
    i = jnp.arange(chunk, dtype=f32)
    dist = i[:, None] - i[None, :]
    intra = jnp.where(dist >= 0, jnp.exp(log_g[:, None, None] * jnp.maximum(dist, 0.0)), 0.0)
    q_decay = jnp.exp(log_g[None, :] * (i[:, None] + 1.0))
    k_decay = jnp.exp(log_g[None, :] * (chunk - 1.0 - i[:, None]))
    chunk_decay = jnp.exp(log_g * chunk)

    def step(s_prev, blk):
        qc, kc, vc = blk
        scores = jnp.einsum('bqhd,bkhd->bhqk', qc, kc) * intra
        o = (jnp.einsum('bhqk,bkhe->bqhe', scores, vc)
             + jnp.einsum('bqhd,bhde->bqhe', qc * q_decay[None, :, :, None], s_prev))
        s_new = (chunk_decay[None, :, None, None] * s_prev
                 + jnp.einsum('bkhd,bkhe->bhde', kc * k_decay[None, :, :, None], vc))
        return s_new, o

    def to_chunks(t):
        return t.reshape(B, n_chunks, chunk, RET_HEADS, t.shape[-1]).transpose(1, 0, 2, 3, 4)

    s_fin, o = lax.scan(step, state.astype(f32), (to_chunks(q), to_chunks(k), to_chunks(v)))
    o = o.transpose(1, 0, 2, 3, 4).reshape(B, S, RET_HEADS, RET_HDV)
    o = layer_norm_f32(o)
    out = jax.nn.silu(g.astype(f32)) * o.reshape(B, S, D_RET_V)
    return out, s_fin


def hier_moe(x, w_group, b_group, w_router, b_router, w_gate, w_up, w_down):
    B, S, D = x.shape
    xt = x.reshape(B * S, D)
    n_tok = xt.shape[0]
    grp_logits = (xt @ w_group).astype(jnp.float32) + b_group.astype(jnp.float32)
    grp_sel = jnp.argmax(grp_logits, axis=-1)
    grp_gate = jnp.take_along_axis(jax.nn.softmax(grp_logits, axis=-1), grp_sel[:, None], axis=1)[:, 0]
    exp_logits = ((xt @ w_router).astype(jnp.float32) + b_router.astype(jnp.float32)).reshape(
        n_tok, N_GROUPS, EXPERTS_PER_GROUP)
    in_grp = jnp.take_along_axis(exp_logits, grp_sel[:, None, None], axis=1)[:, 0]
    top_val, top_idx = lax.top_k(in_grp, TOP_K_IN_GROUP)
    gate = grp_gate[:, None] * jax.nn.softmax(top_val, axis=-1)
    expert = grp_sel[:, None] * EXPERTS_PER_GROUP + top_idx
    n_asg = n_tok * TOP_K_IN_GROUP
    e_flat = expert.reshape(-1)
    tok_flat = jnp.repeat(jnp.arange(n_tok), TOP_K_IN_GROUP)
    order = jnp.argsort(e_flat)
    e_s = e_flat[order]
    tok_s = tok_flat[order]
    gate_s = gate.reshape(-1)[order]
    counts = jnp.bincount(e_flat, length=N_EXPERTS)
    start = jnp.cumsum(counts) - counts
    padded = (counts + MOE_BLOCK - 1) // MOE_BLOCK * MOE_BLOCK
    pend = jnp.cumsum(padded)
    pstart = pend - padded
    dest = pstart[e_s] + jnp.arange(n_asg) - start[e_s]
    n_blk = -(-n_asg // MOE_BLOCK) + N_EXPERTS
    rows = n_blk * MOE_BLOCK
    x_buf = jnp.zeros((rows, D), x.dtype).at[dest].set(xt[tok_s])
    blk_expert = jnp.minimum(jnp.searchsorted(pend, jnp.arange(n_blk) * MOE_BLOCK, side='right'), N_EXPERTS - 1)

    def expert_block(args):
        xb, e = args
        h = jax.nn.silu(xb @ w_gate[e]) * (xb @ w_up[e])
        return h @ w_down[e]

    y_buf = lax.map(expert_block, (x_buf.reshape(n_blk, MOE_BLOCK, D), blk_expert)).reshape(rows, D)
    y = jnp.zeros((n_tok, D), jnp.float32).at[tok_s].add(y_buf[dest].astype(jnp.float32) * gate_s[:, None])
    return y.astype(x.dtype).reshape(B, S, D)


def trunk_layer(x, c, pos, k_buf, v_buf, conv_buf, ret_state, rel_bias_table,
                norm1_g, norm2_g, w_ada, b_ada, w_in, w_out, conv_w, conv_b, conv_ln_g, conv_ln_b,
                moe_w_group, moe_b_group, moe_w_router, moe_b_router, moe_w_gate, moe_w_up, moe_w_down):
    B, S, _ = x.shape
    sh1, sc1, g1, sh2, sc2, g2 = ada_modulation(c, w_ada, b_ada)
    h = rms_norm(x, norm1_g) * (1.0 + sc1) + sh1
    u = h @ w_in
    q_a, k_a, v_a, u_c, q_r, k_r, v_r, g_r = jnp.split(u, list(np.cumsum(IN_SIZES)[:-1]), axis=-1)
    q_a = q_a.reshape(B, S, ATT_HEADS, ATT_HD)
    k_a = k_a.reshape(B, S, ATT_HEADS, ATT_HD)
    v_a = v_a.reshape(B, S, ATT_HEADS, ATT_HD)
    if k_buf is None:
        att = dilated_attention_prompt(q_a, k_a, v_a, rel_bias_table)
        keep = min(MAX_WINDOW, S)
        new_k, new_v = k_a[:, S - keep:], v_a[:, S - keep:]
        conv_buf = jnp.zeros((B, CONV_WIDTH - 1, CONV_CH), x.dtype)
        ret_state = jnp.zeros((B, RET_HEADS, RET_HDK, RET_HDV), jnp.float32)
    else:
        att = dilated_attention_sample(q_a, k_a, v_a, k_buf, v_buf, rel_bias_table)
        new_k, new_v = k_a, v_a
    conv_out, new_conv = conformer_conv(u_c, conv_buf, conv_w, conv_b, conv_ln_g, conv_ln_b)
    ret_out, new_ret = retention(q_r, k_r, v_r, g_r, ret_state, pos)
    mix = jnp.concatenate([att.reshape(B, S, D_ATT), conv_out, ret_out.astype(x.dtype)], axis=-1) @ w_out
    x = x + g1 * mix
    h2 = rms_norm(x, norm2_g) * (1.0 + sc2) + sh2
    x = x + g2 * hier_moe(h2, moe_w_group, moe_b_group, moe_w_router, moe_b_router, moe_w_gate, moe_w_up, moe_w_down)
    return x, new_k, new_v, new_conv, new_ret


def setup_inputs(seed: int = 0) -> dict:
    key = jax.random.key(seed)
    keys = iter(jax.random.split(key, 40))

    def nrm(shape, scale):
        return jax.random.normal(next(keys), shape, jnp.float32) * scale

    att_buf = min(MAX_WINDOW, PAST_LEN)
    return {
        'x_prompt': nrm((BATCH, SEQ, D_MODEL), 1.0),
        'x_sample': nrm((DEC_BATCH, DEC_SEQ, D_MODEL), 1.0),
        'c_prompt': nrm((BATCH, D_MODEL), 1.0),
        'c_sample': nrm((DEC_BATCH, D_MODEL), 1.0),
        'cache_attn_k': nrm((DEPTH, DEC_BATCH, att_buf, ATT_HEADS, ATT_HD), 1.0),
        'cache_attn_v': nrm((DEPTH, DEC_BATCH, att_buf, ATT_HEADS, ATT_HD), 1.0),
        'state_conv': nrm((DEPTH, DEC_BATCH, CONV_WIDTH - 1, CONV_CH), 0.5),
        'state_ret': nrm((DEPTH, DEC_BATCH, RET_HEADS, RET_HDK, RET_HDV), 1.0),
        'rel_bias_table': nrm((REL_BUCKETS, ATT_HEADS), 0.5),
        'norm1_g': 1.0 + nrm((DEPTH, D_MODEL), 0.02),
        'norm2_g': 1.0 + nrm((DEPTH, D_MODEL), 0.02),
        'w_ada': nrm((DEPTH, D_MODEL, 6 * D_MODEL), 0.5 * D_MODEL ** -0.5),
        'b_ada': nrm((DEPTH, 6 * D_MODEL), 0.02),
        'w_in': nrm((DEPTH, D_MODEL, D_IN), D_MODEL ** -0.5),
        'w_out': nrm((DEPTH, D_MIX, D_MODEL), D_MIX ** -0.5),
        'conv_w': nrm((DEPTH, CONV_WIDTH, CONV_CH), CONV_WIDTH ** -0.5),
        'conv_b': nrm((DEPTH, CONV_CH), 0.02),
        'conv_ln_g': 1.0 + nrm((DEPTH, CONV_CH), 0.02),
        'conv_ln_b': nrm((DEPTH, CONV_CH), 0.02),
        'moe_w_group': nrm((DEPTH, D_MODEL, N_GROUPS), D_MODEL ** -0.5),
        'moe_b_group': nrm((DEPTH, N_GROUPS), 0.01),
        'moe_w_router': nrm((DEPTH, D_MODEL, N_EXPERTS), D_MODEL ** -0.5),
        'moe_b_router': nrm((DEPTH, N_EXPERTS), 0.01),
        'moe_w_gate': nrm((DEPTH, N_EXPERTS, D_MODEL, D_EXPERT), D_MODEL ** -0.5),
        'moe_w_up': nrm((DEPTH, N_EXPERTS, D_MODEL, D_EXPERT), D_MODEL ** -0.5),
        'moe_w_down': nrm((DEPTH, N_EXPERTS, D_EXPERT, D_MODEL), D_EXPERT ** -0.5),
        'final_norm_g': 1.0 + nrm((D_MODEL,), 0.02),
    }


def reference(x_prompt, x_sample, c_prompt, c_sample, cache_attn_k, cache_attn_v, state_conv, state_ret,
              rel_bias_table, norm1_g, norm2_g, w_ada, b_ada, w_in, w_out, conv_w, conv_b, conv_ln_g, conv_ln_b,
              moe_w_group, moe_b_group, moe_w_router, moe_b_router, moe_w_gate, moe_w_up, moe_w_down,
              final_norm_g):
    pos_p = jnp.arange(x_prompt.shape[1], dtype=jnp.int32)
    pos_s = PAST_LEN + jnp.arange(x_sample.shape[1], dtype=jnp.int32)
    xp, xs = x_prompt, x_sample
    pk, pv, pc, pr, sk, sv, sc, sr = [], [], [], [], [], [], [], []
    for l in range(DEPTH):
        lw = (norm1_g[l], norm2_g[l], w_ada[l], b_ada[l], w_in[l], w_out[l], conv_w[l], conv_b[l],
              conv_ln_g[l], conv_ln_b[l], moe_w_group[l], moe_b_group[l], moe_w_router[l], moe_b_router[l],
              moe_w_gate[l], moe_w_up[l], moe_w_down[l])
        xp, k_n, v_n, c_n, r_n = trunk_layer(xp, c_prompt, pos_p, None, None, None, None, rel_bias_table, *lw)
        pk.append(k_n); pv.append(v_n); pc.append(c_n); pr.append(r_n)
        xs, k_n, v_n, c_n, r_n = trunk_layer(xs, c_sample, pos_s, cache_attn_k[l], cache_attn_v[l],
                                             state_conv[l], state_ret[l], rel_bias_table, *lw)
        sk.append(k_n); sv.append(v_n); sc.append(c_n); sr.append(r_n)
    y_prompt = rms_norm(xp, final_norm_g)
    y_sample = rms_norm(xs, final_norm_g)
    return (y_prompt, y_sample, jnp.stack(pk), jnp.stack(pv), jnp.stack(pc), jnp.stack(pr),
            jnp.stack(sk), jnp.stack(sv), jnp.stack(sc), jnp.stack(sr))
```

```python
import functools
import math

import numpy as np
import jax
import jax.numpy as jnp
from jax import lax
from jax.experimental import pallas as pl
from jax.experimental.pallas import tpu as pltpu

F32 = jnp.float32
BF16 = jnp.bfloat16

D_MODEL = 1024
ATT_HEADS = 8
ATT_HD = 64
D_ATT = ATT_HEADS * ATT_HD
BRANCHES = ((128, 1), (512, 4), (2048, 16))
STEPS = 128
REL_BUCKETS = 32
REL_MAX_DIST = 2048
CONV_CH = 256
CONV_WIDTH = 31
HALO = 32
RET_HEADS = 4
RET_HDK = 32
RET_HDV = 64
D_RET_K = RET_HEADS * RET_HDK
D_RET_V = RET_HEADS * RET_HDV
RET_CHUNK = 128
N_GROUPS = 4
EXPERTS_PER_GROUP = 8
N_EXPERTS = N_GROUPS * EXPERTS_PER_GROUP
D_EXPERT = 512
NORM_EPS = 1e-6
LN_EPS = 1e-5
NEG_INF = -1e30
LANES = 128
IN_COLS = ((0, 512), (512, 1024), (1024, 1536), (1536, 2048), (2048, 2304), (2304, 2560), (2560, 2816))
D_IN = 2816
VMEM_LIMIT = 48 * 1024 * 1024

NT = (((1,), (1,)), ((), ()))
TN = (((0,), (0,)), ((), ()))


def _params(*sem):
    return pltpu.CompilerParams(dimension_semantics=sem, vmem_limit_bytes=VMEM_LIMIT)


def _dot(a, b):
    return jnp.dot(a, b, preferred_element_type=F32)


def _split_bf16(a):
    hi = a.astype(BF16)
    return hi, (a - hi.astype(F32)).astype(BF16)


def _dot3(a, b):
    ah, al = _split_bf16(a)
    bh, bl = _split_bf16(b)
    return _dot(ah, bh) + (_dot(al, bh) + _dot(ah, bl))


def _sigmoid(x):
    return 1.0 / (1.0 + jnp.exp(-x))


def _ada_kernel(c_ref, w_ref, b_ref, o_ref):
    c = c_ref[...]
    o_ref[0] = _dot3(c * _sigmoid(c), w_ref[0]) + b_ref[0]


def _ada(c_all, w_ada, b_ada):
    depth, d, n6 = w_ada.shape
    bc = c_all.shape[0]
    tn = 1536
    return pl.pallas_call(
        _ada_kernel,
        grid=(depth, n6 // tn),
        in_specs=[pl.BlockSpec((bc, d), lambda l, j: (0, 0)),
                  pl.BlockSpec((1, d, tn), lambda l, j: (l, 0, j)),
                  pl.BlockSpec((1, 1, tn), lambda l, j: (l, 0, j))],
        out_specs=pl.BlockSpec((1, bc, tn), lambda l, j: (l, 0, j)),
        out_shape=jax.ShapeDtypeStruct((depth, bc, n6), F32),
        compiler_params=_params("arbitrary", "arbitrary"),
        name="ada",
    )(c_all, w_ada, b_ada.reshape(depth, 1, n6))


def _inproj_kernel(x_ref, sh_ref, sc_ref, g_ref, w_ref, *out_refs):
    x = x_ref[0]
    ms = jnp.mean(x * x, axis=-1, keepdims=True)
    y = x * lax.rsqrt(ms + NORM_EPS) * g_ref[...]
    h = (y * (1.0 + sc_ref[0]) + sh_ref[0]).astype(BF16)
    for ref, (a, b) in zip(out_refs, IN_COLS):
        ref[0] = _dot(h, w_ref[:, a:b])


def _inproj(x, sh, sc, g, w_bf16, ts):
    b, s, d = x.shape
    mr = sh.shape[1]
    mod_map = (lambda bi, i: (bi, 0, 0)) if mr == 1 else (lambda bi, i: (bi, i, 0))
    mod_blk = (1, 1, d) if mr == 1 else (1, ts, d)
    widths = [c1 - c0 for c0, c1 in IN_COLS]
    return pl.pallas_call(
        _inproj_kernel,
        grid=(b, s // ts),
        in_specs=[pl.BlockSpec((1, ts, d), lambda bi, i: (bi, i, 0)),
                  pl.BlockSpec(mod_blk, mod_map),
                  pl.BlockSpec(mod_blk, mod_map),
                  pl.BlockSpec((1, d), lambda bi, i: (0, 0)),
                  pl.BlockSpec((d, D_IN), lambda bi, i: (0, 0))],
        out_specs=[pl.BlockSpec((1, ts, w), lambda bi, i: (bi, i, 0)) for w in widths],
        out_shape=[jax.ShapeDtypeStruct((b, s, w), F32) for w in widths],
        compiler_params=_params("parallel", "parallel"),
        name="inproj",
    )(x, sh, sc, g.reshape(1, d), w_bf16)


def _t5_bucket(dist):
    n = np.asarray(dist)
    max_exact = REL_BUCKETS // 2
    large = max_exact + (np.log(np.maximum(n, 1) / max_exact) / math.log(REL_MAX_DIST / max_exact)
                         * (REL_BUCKETS - max_exact)).astype(np.int64)
    large = np.minimum(large, REL_BUCKETS - 1)
    return np.where(n < max_exact, n, large).astype(np.int32)


def _prompt_bias(rel_bias_table, dil):
    qi = np.arange(STEPS)[:, None]
    kj = np.arange(2 * STEPS)[None, :]
    tabs = []
    for koff in (0, STEPS):
        delta = qi - kj + koff
        valid = (delta >= 0) & (delta <= STEPS)
        bucket = _t5_bucket(dil * np.clip(delta, 0, STEPS))
        bias = jnp.transpose(rel_bias_table[bucket], (2, 0, 1)).astype(F32)
        tabs.append(jnp.where(valid[None], bias, NEG_INF))
    t = jnp.stack(tabs, axis=1)
    return t.reshape(ATT_HEADS // 2, 2, 2, STEPS, 2 * STEPS)


def _sample_bias(rel_bias_table, w, t_new):
    loc = np.arange(w + t_new)[:, None]
    tq = np.arange(t_new)[None, :]
    delta = w + tq - loc
    tabs = []
    for window, dil in BRANCHES:
        valid = (delta >= 0) & (delta % dil == 0) & (delta <= window)
        bucket = _t5_bucket(np.clip(delta, 0, window))
        bias = rel_bias_table[bucket].astype(F32)
        bias = jnp.where(valid[:, :, None], bias, NEG_INF)
        tabs.append(jnp.transpose(bias, (0, 2, 1)).reshape(w + t_new, ATT_HEADS * t_new))
    t = jnp.stack(tabs, axis=0)
    return t[:, :w], t[:, w:]


def _attn_branch_kernel(q_ref, k_ref, v_ref, bias_ref, o_ref, l_ref, *, nj):
    lane = lax.broadcasted_iota(jnp.int32, (1, LANES), 1)

    def tile(jt, carry):
        qs = pl.multiple_of(jt * STEPS, STEPS)
        ks = pl.multiple_of(jnp.maximum(jt - 1, 0) * STEPS, STEPS)
        var = jnp.minimum(jt, 1)
        q = q_ref[0, pl.ds(qs, STEPS), :] * (ATT_HD ** -0.5)
        k = k_ref[0, pl.ds(ks, 2 * STEPS), :].astype(BF16)
        v = v_ref[0, pl.ds(ks, 2 * STEPS), :].astype(BF16)
        o = jnp.zeros((STEPS, LANES), F32)
        lse = jnp.zeros((STEPS, LANES), F32)
        for hh in range(2):
            msk = (lane >= ATT_HD) if hh else (lane < ATT_HD)
            qh = jnp.where(msk, q, 0.0).astype(BF16)
            s = lax.dot_general(qh, k, NT, preferred_element_type=F32) + bias_ref[0, hh, var]
            m = jnp.max(s, axis=-1, keepdims=True)
            p = jnp.exp(s - m)
            l = jnp.sum(p, axis=-1, keepdims=True)
            pv = _dot(p.astype(BF16), v)
            o = jnp.where(msk, pv * (1.0 / l), o)
            lse = jnp.where(msk, m + jnp.log(l), lse)
        o_ref[0, pl.ds(qs, STEPS), :] = o
        l_ref[0, pl.ds(qs, STEPS), :] = lse
        return carry

    lax.fori_loop(0, nj, tile, 0)


def _attn_branch(q, k, v, bias, dil):
    b, s, _ = q.shape
    sd = s // dil
    assert s % (dil * 2 * STEPS) == 0, "prompt length must cover two 128-step tiles per residue"
    view = lambda t: t.reshape(b, sd, dil * D_ATT)
    hp = ATT_HEADS // 2
    blk = pl.BlockSpec((1, sd, LANES), lambda bi, r, h: (bi, 0, r * hp + h))
    o, l = pl.pallas_call(
        functools.partial(_attn_branch_kernel, nj=sd // STEPS),
        grid=(b, dil, hp),
        in_specs=[blk, blk, blk,
                  pl.BlockSpec((1, 2, 2, STEPS, 2 * STEPS), lambda bi, r, h: (h, 0, 0, 0, 0))],
        out_specs=[blk, blk],
        out_shape=[jax.ShapeDtypeStruct((b, sd, dil * D_ATT), F32)] * 2,
        compiler_params=_params("parallel", "parallel", "parallel"),
        name=f"attn_d{dil}",
    )(view(q), view(k), view(v), bias)
    return o.reshape(b, s, D_ATT), l.reshape(b, s, D_ATT)


def _attn_sample_kernel(qbd_ref, kc_ref, vc_ref, kn_ref, vn_ref, bc_ref, bn_ref, o_ref, *, t_new):
    qbd = (qbd_ref[0] * (ATT_HD ** -0.5)).astype(BF16)
    sc = _dot(kc_ref[0].astype(BF16), qbd)
    sn = _dot(kn_ref[0].astype(BF16), qbd)
    nb = len(BRANCHES)
    m = None
    for i in range(nb):
        mi = jnp.maximum(jnp.max(sc + bc_ref[i], axis=0, keepdims=True),
                         jnp.max(sn + bn_ref[i], axis=0, keepdims=True))
        m = mi if m is None else jnp.maximum(m, mi)
    pc = jnp.exp(sc + bc_ref[0] - m)
    pn = jnp.exp(sn + bn_ref[0] - m)
    for i in range(1, nb):
        pc = pc + jnp.exp(sc + bc_ref[i] - m)
        pn = pn + jnp.exp(sn + bn_ref[i] - m)
    pc = pc.astype(BF16)
    pn = pn.astype(BF16)
    w = pc.shape[0]
    ob = (lax.dot_general(pc, vc_ref[0].astype(BF16), TN, preferred_element_type=F32)
          + lax.dot_general(pn, vn_ref[0].astype(BF16), TN, preferred_element_type=F32))
    lb = (lax.dot_general(pc, jnp.ones((w, LANES), BF16), TN, preferred_element_type=F32)
          + lax.dot_general(pn, jnp.ones((t_new, LANES), BF16), TN, preferred_element_type=F32))
    ob = ob * (1.0 / lb[:, 0:1])
    lane_head = lax.broadcasted_iota(jnp.int32, (1, D_ATT), 1) // ATT_HD
    out = jnp.zeros((t_new, D_ATT), F32)
    for h in range(ATT_HEADS):
        out = jnp.where(lane_head == h, ob[h * t_new:(h + 1) * t_new, :], out)
    o_ref[0] = out


def _attn_sample(q, k_new, v_new, k_buf, v_buf, bias_c, bias_n):
    bd, t_new, _ = q.shape
    w = k_buf.shape[1]
    assert t_new == 8, "lane layout head*T+t assumes 8 new positions"
    ht = ATT_HEADS * t_new
    qh = q.reshape(bd, t_new, ATT_HEADS, ATT_HD)
    qbd = jnp.einsum("bthc,hg->bhcgt", qh, jnp.eye(ATT_HEADS, dtype=F32)).reshape(bd, D_ATT, ht)
    return pl.pallas_call(
        functools.partial(_attn_sample_kernel, t_new=t_new),
        grid=(bd,),
        in_specs=[pl.BlockSpec((1, D_ATT, ht), lambda b: (b, 0, 0)),
                  pl.BlockSpec((1, w, D_ATT), lambda b: (b, 0, 0)),
                  pl.BlockSpec((1, w, D_ATT), lambda b: (b, 0, 0)),
                  pl.BlockSpec((1, t_new, D_ATT), lambda b: (b, 0, 0)),
                  pl.BlockSpec((1, t_new, D_ATT), lambda b: (b, 0, 0)),
                  pl.BlockSpec((len(BRANCHES), w, ht), lambda b: (0, 0, 0)),
                  pl.BlockSpec((len(BRANCHES), t_new, ht), lambda b: (0, 0, 0))],
        out_specs=pl.BlockSpec((1, t_new, D_ATT), lambda b: (b, 0, 0)),
        out_shape=jax.ShapeDtypeStruct((bd, t_new, D_ATT), F32),
        compiler_params=_params("parallel"),
        name="attn_sample",
    )(qbd, k_buf, v_buf, k_new, v_new, bias_c, bias_n)


def _conv_kernel(*refs, ts, use_halo):
    if use_halo:
        u_ref, hu_ref, st_ref, w_ref, cb_ref, lg_ref, lb_ref, o_ref, ns_ref, hc = refs
    else:
        u_ref, st_ref, w_ref, cb_ref, lg_ref, lb_ref, o_ref, ns_ref, hc = refs
    u = u_ref[0]
    h = u[:, :CONV_CH] * _sigmoid(u[:, CONV_CH:])
    halo = st_ref[0]
    if use_halo:
        hu = hu_ref[0]
        halo = jnp.where(pl.program_id(1) == 0, halo, hu[:, :CONV_CH] * _sigmoid(hu[:, CONV_CH:]))
    hc[0:HALO, :] = halo
    hc[HALO:HALO + ts, :] = h
    rows = min(ts, 64)
    off = HALO - (CONV_WIDTH - 1)
    for r0 in range(0, ts, rows):
        acc = jnp.broadcast_to(cb_ref[...], (rows, CONV_CH))
        for j in range(CONV_WIDTH):
            acc = acc + w_ref[j:j + 1, :] * hc[r0 + off + j:r0 + off + j + rows, :]
        mu = jnp.mean(acc, axis=-1, keepdims=True)
        xc = acc - mu
        var = jnp.mean(xc * xc, axis=-1, keepdims=True)
        y = xc * lax.rsqrt(var + LN_EPS) * lg_ref[...] + lb_ref[...]
        o_ref[0, r0:r0 + rows, :] = y * _sigmoid(y)
    ns_ref[0] = hc[ts:ts + HALO, :]


def _conv(u_c, state, conv_w, conv_b, ln_g, ln_b, ts):
    b, s, _ = u_c.shape
    n_t = s // ts
    use_halo = n_t > 1
    st = jnp.pad(state, ((0, 0), (HALO - (CONV_WIDTH - 1), 0), (0, 0)))
    w = jnp.pad(conv_w, ((0, HALO - CONV_WIDTH), (0, 0)))
    vec = lambda a: a.reshape(1, CONV_CH)
    in_specs = [pl.BlockSpec((1, ts, 2 * CONV_CH), lambda bi, i: (bi, i, 0))]
    args = [u_c]
    if use_halo:
        per = ts // HALO
        in_specs.append(pl.BlockSpec((1, HALO, 2 * CONV_CH), lambda bi, i: (bi, jnp.maximum(i * per - 1, 0), 0)))
        args.append(u_c)
    in_specs += [pl.BlockSpec((1, HALO, CONV_CH), lambda bi, i: (bi, 0, 0)),
                 pl.BlockSpec((HALO, CONV_CH), lambda bi, i: (0, 0))]
    in_specs += [pl.BlockSpec((1, CONV_CH), lambda bi, i: (0, 0))] * 3
    args += [st, w, vec(conv_b), vec(ln_g), vec(ln_b)]
    out, ns = pl.pallas_call(
        functools.partial(_conv_kernel, ts=ts, use_halo=use_halo),
        grid=(b, n_t),
        in_specs=in_specs,
        out_specs=[pl.BlockSpec((1, ts, CONV_CH), lambda bi, i: (bi, i, 0)),
                   pl.BlockSpec((1, HALO, CONV_CH), lambda bi, i: (bi, 0, 0))],
        out_shape=[jax.ShapeDtypeStruct((b, s, CONV_CH), F32),
                   jax.ShapeDtypeStruct((b, HALO, CONV_CH), F32)],
        scratch_shapes=[pltpu.VMEM((ts + HALO, CONV_CH), F32)],
        compiler_params=_params("parallel", "arbitrary"),
        name="conv",
    )(*args)
    return out, ns[:, HALO - (CONV_WIDTH - 1):]


def _ret_kernel(qk_ref, v_ref, g_ref, cos_ref, sin_ref, intra_ref, qd_ref, kd_ref, cd_ref, bm_ref, s0_ref,
                o_ref, so_ref, s_scr):
    @pl.when(pl.program_id(1) == 0)
    def _():
        s_scr[...] = s0_ref[0]

    qk = qk_ref[0]
    cos = cos_ref[...]
    sin = sin_ref[...]
    lane = lax.broadcasted_iota(jnp.int32, (1, D_RET_K), 1)
    first_half = (lane % RET_HDK) < (RET_HDK // 2)

    def rot(x):
        swapped = jnp.where(first_half, pltpu.roll(x, D_RET_K - RET_HDK // 2, 1), pltpu.roll(x, RET_HDK // 2, 1))
        return x * cos + swapped * sin

    q = rot(qk[:, :D_RET_K]) * (RET_HDK ** -0.5)
    k = rot(qk[:, D_RET_K:])
    kb = k.astype(BF16)
    vb = v_ref[0].astype(BF16)
    state = s_scr[...]
    lane_q = lane // RET_HDK
    lane_v = lax.broadcasted_iota(jnp.int32, (1, D_RET_V), 1) // RET_HDV
    o = _dot((q * qd_ref[...]).astype(BF16), state.astype(BF16))
    for h in range(RET_HEADS):
        qh = jnp.where(lane_q == h, q, 0.0).astype(BF16)
        scores = lax.dot_general(qh, kb, NT, preferred_element_type=F32) * intra_ref[h]
        o = o + jnp.where(lane_v == h, _dot(scores.astype(BF16), vb), 0.0)
    kv = lax.dot_general((k * kd_ref[...]).astype(BF16), vb, TN, preferred_element_type=F32)
    new_state = cd_ref[...] * state + bm_ref[...] * kv
    s_scr[...] = new_state
    so_ref[0] = new_state
    res = jnp.zeros_like(o)
    for h in range(RET_HEADS):
        mh = lane_v == h
        mu = jnp.sum(jnp.where(mh, o, 0.0), axis=-1, keepdims=True) * (1.0 / RET_HDV)
        xc = jnp.where(mh, o - mu, 0.0)
        var = jnp.sum(xc * xc, axis=-1, keepdims=True) * (1.0 / RET_HDV)
        res = res + xc * lax.rsqrt(var + LN_EPS)
    g = g_ref[0]
    o_ref[0] = g * _sigmoid(g) * res


def _retention(qk, v, g, state, pos):
    b, s, _ = qk.shape
    chunk = RET_CHUNK if s % RET_CHUNK == 0 else s
    n_c = s // chunk
    half = RET_HDK // 2
    inv = 1.0 / (10000.0 ** jnp.linspace(0.0, 1.0, half, dtype=F32))
    ang = pos.astype(F32)[:, None] * inv[None, :]
    cos = jnp.tile(jnp.cos(ang), (1, 2 * RET_HEADS))
    sin = jnp.tile(jnp.concatenate([-jnp.sin(ang), jnp.sin(ang)], axis=1), (1, RET_HEADS))
    log_g = jnp.log1p(-jnp.exp2(-5.0 - jnp.arange(RET_HEADS, dtype=F32)))
    i = jnp.arange(chunk, dtype=F32)
    dist = i[:, None] - i[None, :]
    intra = jnp.where(dist >= 0, jnp.exp(log_g[:, None, None] * jnp.maximum(dist, 0.0)), 0.0)
    q_decay = jnp.repeat(jnp.exp(log_g[None, :] * (i[:, None] + 1.0)), RET_HDK, axis=1)
    k_decay = jnp.repeat(jnp.exp(log_g[None, :] * (chunk - 1.0 - i[:, None])), RET_HDK, axis=1)
    cd = jnp.broadcast_to(jnp.repeat(jnp.exp(log_g * chunk), RET_HDK)[:, None], (D_RET_K, D_RET_V))
    eye = jnp.eye(RET_HEADS, dtype=F32)
    bm = jnp.repeat(jnp.repeat(eye, RET_HDK, axis=0), RET_HDV, axis=1)
    s0 = jnp.einsum("bhde,hg->bhdge", state.astype(F32), eye).reshape(b, D_RET_K, D_RET_V)
    seq = lambda w: pl.BlockSpec((1, chunk, w), lambda bi, c: (bi, c, 0))
    tab = lambda r, w: pl.BlockSpec((r, w), lambda bi, c: (0, 0))
    out, s_fin = pl.pallas_call(
        _ret_kernel,
        grid=(b, n_c),
        in_specs=[seq(2 * D_RET_K), seq(D_RET_V), seq(D_RET_V),
                  pl.BlockSpec((chunk, D_RET_K), lambda bi, c: (c, 0)),
                  pl.BlockSpec((chunk, D_RET_K), lambda bi, c: (c, 0)),
                  pl.BlockSpec((RET_HEADS, chunk, chunk), lambda bi, c: (0, 0, 0)),
                  tab(chunk, D_RET_K), tab(chunk, D_RET_K), tab(D_RET_K, D_RET_V), tab(D_RET_K, D_RET_V),
                  pl.BlockSpec((1, D_RET_K, D_RET_V), lambda bi, c: (bi, 0, 0))],
        out_specs=[seq(D_RET_V), pl.BlockSpec((1, D_RET_K, D_RET_V), lambda bi, c: (bi, 0, 0))],
        out_shape=[jax.ShapeDtypeStruct((b, s, D_RET_V), F32),
                   jax.ShapeDtypeStruct((b, D_RET_K, D_RET_V), F32)],
        scratch_shapes=[pltpu.VMEM((D_RET_K, D_RET_V), F32)],
        compiler_params=_params("parallel", "arbitrary"),
        name="retention",
    )(qk, v, g, cos, sin, intra, q_decay, k_decay, cd, bm, s0)
    s5 = s_fin.reshape(b, RET_HEADS, RET_HDK, RET_HEADS, RET_HDV)
    new_state = jnp.stack([s5[:, h, :, h, :] for h in range(RET_HEADS)], axis=1)
    return out, new_state


ROUTER_COL = 0
GROUP_COL = N_EXPERTS


def _route(logits):
    lane = lax.broadcasted_iota(jnp.int32, (1, LANES), 1)
    lane_f = lane.astype(F32)
    is_grp = (lane >= GROUP_COL) & (lane < GROUP_COL + N_GROUPS)
    gl = jnp.where(is_grp, logits, -jnp.inf)
    gmax = jnp.max(gl, axis=-1, keepdims=True)
    gsel = jnp.min(jnp.where(gl == gmax, lane_f, 1e9), axis=-1, keepdims=True) - float(GROUP_COL)
    gsum = jnp.sum(jnp.exp(gl - gmax), axis=-1, keepdims=True)
    grp_gate = 1.0 / gsum
    in_grp = (lane < N_EXPERTS) & ((lane // EXPERTS_PER_GROUP).astype(F32) == gsel)
    el = jnp.where(in_grp, logits, -jnp.inf)
    t1 = jnp.max(el, axis=-1, keepdims=True)
    i1 = jnp.min(jnp.where(el == t1, lane_f, 1e9), axis=-1, keepdims=True)
    el2 = jnp.where(lane_f == i1, -jnp.inf, el)
    t2 = jnp.max(el2, axis=-1, keepdims=True)
    i2 = jnp.min(jnp.where(el2 == t2, lane_f, 1e9), axis=-1, keepdims=True)
    a = jnp.exp(t2 - t1)
    g0 = grp_gate * (1.0 / (1.0 + a))
    g1 = grp_gate * (a / (1.0 + a))
    eid = jnp.where(lane == 0, i1, jnp.where(lane == 1, i2, 0.0)).astype(jnp.int32)
    gate = jnp.where(lane == 0, g0, jnp.where(lane == 1, g1, 0.0))
    return eid, gate


def _outproj_kernel(*refs, merged):
    if merged:
        (x_ref, att_ref, cv_ref, rt_ref, g1_ref, sh2_ref, sc2_ref, n2_ref, wo_ref, wr_ref, br_ref,
         xo_ref, h2_ref, eid_ref, gate_ref) = refs
        att = att_ref[0]
    else:
        (x_ref, o1, o2, o3, l1, l2, l3, cv_ref, rt_ref, g1_ref, sh2_ref, sc2_ref, n2_ref, wo_ref, wr_ref,
         br_ref, xo_ref, h2_ref, eid_ref, gate_ref) = refs
        la, lb, lc = l1[0], l2[0], l3[0]
        m = jnp.maximum(jnp.maximum(la, lb), lc)
        ea, eb, ec = jnp.exp(la - m), jnp.exp(lb - m), jnp.exp(lc - m)
        att = (ea * o1[0] + eb * o2[0] + ec * o3[0]) * (1.0 / (ea + eb + ec))
    mix = (_dot(att.astype(BF16), wo_ref[0:D_ATT, :])
           + _dot(cv_ref[0].astype(BF16), wo_ref[D_ATT:D_ATT + CONV_CH, :])
           + _dot(rt_ref[0].astype(BF16), wo_ref[D_ATT + CONV_CH:, :]))
    x = x_ref[0] + g1_ref[0] * mix
    xo_ref[0] = x
    ms = jnp.mean(x * x, axis=-1, keepdims=True)
    h2 = x * lax.rsqrt(ms + NORM_EPS) * n2_ref[...] * (1.0 + sc2_ref[0]) + sh2_ref[0]
    h2_ref[0] = h2
    eid, gate = _route(_dot3(h2, wr_ref[...]) + br_ref[...])
    eid_ref[0] = eid
    gate_ref[0] = gate


def _outproj(x, att_parts, conv_out, ret_out, g1, sh2, sc2, n2g, wo_bf16, w_rt, b_rt, ts):
    b, s, d = x.shape
    merged = len(att_parts) == 1
    mr = g1.shape[1]
    mod_map = (lambda bi, i: (bi, 0, 0)) if mr == 1 else (lambda bi, i: (bi, i, 0))
    mod_blk = (1, 1, d) if mr == 1 else (1, ts, d)
    seq = lambda w: pl.BlockSpec((1, ts, w), lambda bi, i: (bi, i, 0))
    const = lambda r, w: pl.BlockSpec((r, w), lambda bi, i: (0, 0))
    in_specs = ([seq(d)] + [seq(D_ATT)] * len(att_parts) + [seq(CONV_CH), seq(D_RET_V)]
                + [pl.BlockSpec(mod_blk, mod_map)] * 3
                + [const(1, d), const(d, d), const(d, LANES), const(1, LANES)])
    return pl.pallas_call(
        functools.partial(_outproj_kernel, merged=merged),
        grid=(b, s // ts),
        in_specs=in_specs,
        out_specs=[seq(d), seq(d), seq(LANES), seq(LANES)],
        out_shape=[jax.ShapeDtypeStruct((b, s, d), F32), jax.ShapeDtypeStruct((b, s, d), F32),
                   jax.ShapeDtypeStruct((b, s, LANES), jnp.int32), jax.ShapeDtypeStruct((b, s, LANES), F32)],
        compiler_params=_params("parallel", "parallel"),
        name="outproj",
    )(x, *att_parts, conv_out, ret_out, g1, sh2, sc2, n2g.reshape(1, d), wo_bf16, w_rt, b_rt)


def _gather_kernel(idx_ref, x_hbm, o_ref, sem, *, rows):
    def issue(r, carry):
        t = idx_ref[0, 0, r]
        pltpu.make_async_copy(x_hbm.at[pl.ds(t, 1)], o_ref.at[pl.ds(r, 1)], sem).start()
        return carry

    lax.fori_loop(0, rows, issue, 0)

    def drain(r, carry):
        pltpu.make_async_copy(x_hbm.at[pl.ds(0, 1)], o_ref.at[pl.ds(r, 1)], sem).wait()
        return carry

    lax.fori_loop(0, rows, drain, 0)


GATHER_ROWS = 512


def _gather_rows(x, idx):
    n_out = idx.shape[0]
    d = x.shape[1]
    rows = max(r for r in range(8, GATHER_ROWS + 1, 8) if n_out % r == 0)
    steps = n_out // rows
    return pl.pallas_call(
        functools.partial(_gather_kernel, rows=rows),
        grid=(steps,),
        in_specs=[pl.BlockSpec((1, 1, rows), lambda i: (i, 0, 0), memory_space=pltpu.SMEM),
                  pl.BlockSpec(memory_space=pl.ANY)],
        out_specs=pl.BlockSpec((rows, d), lambda i: (i, 0)),
        out_shape=jax.ShapeDtypeStruct((n_out, d), x.dtype),
        scratch_shapes=[pltpu.SemaphoreType.DMA(())],
        compiler_params=_params("arbitrary"),
        name="gather_rows",
    )(idx.reshape(steps, 1, rows), x)


def _ffn_kernel(be_ref, nu_ref, x_ref, wg_ref, wu_ref, wd_ref, y_ref):
    i = pl.program_id(0)

    @pl.when(i < nu_ref[0])
    def _():
        x = x_ref[...].astype(BF16)
        g = _dot(x, wg_ref[0].astype(BF16))
        u = _dot(x, wu_ref[0].astype(BF16))
        h = (g * _sigmoid(g) * u).astype(BF16)
        y_ref[...] = _dot(h, wd_ref[0].astype(BF16))

    @pl.when(i >= nu_ref[0])
    def _():
        y_ref[...] = jnp.zeros_like(y_ref)


def _ffn(x_buf, blk_expert, n_used, w_gate, w_up, w_down, bs):
    rows, d = x_buf.shape
    n_blk = rows // bs
    grid_spec = pltpu.PrefetchScalarGridSpec(
        num_scalar_prefetch=2,
        grid=(n_blk,),
        in_specs=[pl.BlockSpec((bs, d), lambda i, be, nu: (i, 0)),
                  pl.BlockSpec((1, d, D_EXPERT), lambda i, be, nu: (be[i], 0, 0)),
                  pl.BlockSpec((1, d, D_EXPERT), lambda i, be, nu: (be[i], 0, 0)),
                  pl.BlockSpec((1, D_EXPERT, d), lambda i, be, nu: (be[i], 0, 0))],
        out_specs=pl.BlockSpec((bs, d), lambda i, be, nu: (i, 0)),
    )
    return pl.pallas_call(
        _ffn_kernel,
        grid_spec=grid_spec,
        out_shape=jax.ShapeDtypeStruct((rows, d), F32),
        compiler_params=_params("arbitrary"),
        name="moe_ffn",
    )(blk_expert, n_used, x_buf, w_gate, w_up, w_down)


def _combine_kernel(*refs, final):
    if final:
        x_ref, y0_ref, y1_ref, gate_ref, g2_ref, fg_ref, xo_ref, yn_ref = refs
    else:
        x_ref, y0_ref, y1_ref, gate_ref, g2_ref, xo_ref = refs
    gate = gate_ref[0]
    y = y0_ref[...] * gate[:, 0:1] + y1_ref[...] * gate[:, 1:2]
    x = x_ref[0] + g2_ref[0] * y
    xo_ref[0] = x
    if final:
        ms = jnp.mean(x * x, axis=-1, keepdims=True)
        yn_ref[0] = x * lax.rsqrt(ms + NORM_EPS) * fg_ref[...]


def _combine(x, yg, gate, g2, final_g, ts):
    b, s, d = x.shape
    n_t = s // ts
    slot1 = b * n_t
    mr = g2.shape[1]
    mod_map = (lambda bi, i: (bi, 0, 0)) if mr == 1 else (lambda bi, i: (bi, i, 0))
    mod_blk = (1, 1, d) if mr == 1 else (1, ts, d)
    seq = lambda w: pl.BlockSpec((1, ts, w), lambda bi, i: (bi, i, 0))
    final = final_g is not None
    in_specs = [seq(d),
                pl.BlockSpec((ts, d), lambda bi, i: (bi * n_t + i, 0)),
                pl.BlockSpec((ts, d), lambda bi, i: (slot1 + bi * n_t + i, 0)),
                seq(LANES), pl.BlockSpec(mod_blk, mod_map)]
    args = [x, yg, yg, gate, g2]
    out_specs = [seq(d)]
    out_shape = [jax.ShapeDtypeStruct((b, s, d), F32)]
    if final:
        in_specs.append(pl.BlockSpec((1, d), lambda bi, i: (0, 0)))
        args.append(final_g.reshape(1, d))
        out_specs.append(seq(d))
        out_shape.append(jax.ShapeDtypeStruct((b, s, d), F32))
    res = pl.pallas_call(
        functools.partial(_combine_kernel, final=final),
        grid=(b, n_t),
        in_specs=in_specs,
        out_specs=out_specs,
        out_shape=out_shape,
        compiler_params=_params("parallel", "parallel"),
        name="moe_combine",
    )(*args)
    return (res[0], res[1]) if final else (res[0], None)


def _dispatch(expert, bs):
    n_tok = expert.shape[0]
    n_asg = 2 * n_tok
    e_flat = expert.reshape(-1)
    order = jnp.argsort(e_flat, stable=True).astype(jnp.int32)
    counts = jnp.sum((e_flat[:, None] == jnp.arange(N_EXPERTS, dtype=jnp.int32)[None, :]).astype(jnp.int32), axis=0)
    start = jnp.cumsum(counts) - counts
    padded = (counts + bs - 1) // bs * bs
    pend = jnp.cumsum(padded)
    pstart = pend - padded
    n_blk = -(-n_asg // bs) + N_EXPERTS
    rows = n_blk * bs
    row = jnp.arange(rows, dtype=jnp.int32)
    row_e = jnp.minimum(jnp.searchsorted(pend, row, side="right"), N_EXPERTS - 1).astype(jnp.int32)
    idx_in = row - pstart[row_e]
    valid = idx_in < counts[row_e]
    src_asg = order[jnp.clip(start[row_e] + idx_in, 0, n_asg - 1)]
    src_tok = jnp.where(valid, src_asg // 2, 0).astype(jnp.int32)
    e_s = e_flat[order]
    dest_sorted = (pstart[e_s] + jnp.arange(n_asg, dtype=jnp.int32) - start[e_s]).astype(jnp.int32)
    dest = jnp.zeros((n_asg,), jnp.int32).at[order].set(dest_sorted)
    blk_expert = row_e[::bs]
    n_used = (pend[-1] // bs).astype(jnp.int32).reshape(1)
    return src_tok, dest, blk_expert, n_used


def _moe(x, h2, eid, gate, g2, w_gate, w_up, w_down, final_g, bs, ts):
    b, s, d = x.shape
    n_tok = b * s
    expert = eid.reshape(n_tok, LANES)[:, :2]
    src_tok, dest, blk_expert, n_used = _dispatch(expert, bs)
    x_buf = _gather_rows(h2.reshape(n_tok, d), src_tok)
    y_buf = _ffn(x_buf, blk_expert, n_used, w_gate, w_up, w_down, bs)
    dest2 = dest.reshape(n_tok, 2).T.reshape(-1)
    yg = _gather_rows(y_buf, dest2)
    return _combine(x, yg, gate, g2, final_g, ts)


def _layer(x, mods, pos, caches, rel_tabs, lw, final_g, ts, bs):
    (norm1_g, norm2_g, w_in_b, w_out_b, conv_w, conv_b, conv_ln_g, conv_ln_b, w_rt, b_rt,
     w_gate, w_up, w_down) = lw
    sh1, sc1, g1, sh2, sc2, g2 = mods
    b, s, d = x.shape
    q_a, k_a, v_a, u_c, qk_r, v_r, g_r = _inproj(x, sh1, sc1, norm1_g, w_in_b, ts)
    if caches is None:
        parts_o, parts_l = [], []
        for (window, dil), bias in zip(BRANCHES, rel_tabs):
            o, l = _attn_branch(q_a, k_a, v_a, bias, dil)
            parts_o.append(o)
            parts_l.append(l)
        att_parts = parts_o + parts_l
        keep = min(BRANCHES[-1][0], s)
        new_k, new_v = k_a[:, s - keep:], v_a[:, s - keep:]
        conv_state = jnp.zeros((b, CONV_WIDTH - 1, CONV_CH), F32)
        ret_state = jnp.zeros((b, RET_HEADS, RET_HDK, RET_HDV), F32)
        q_a3, k_a3, v_a3 = q_a, k_a, v_a
    else:
        k_buf, v_buf, conv_state, ret_state, bd, t_new = caches
        w = k_buf.shape[1]
        shp = lambda t, c: t.reshape(bd, t_new, c)
        att = _attn_sample(shp(q_a, D_ATT), shp(k_a, D_ATT), shp(v_a, D_ATT),
                           k_buf.reshape(bd, w, D_ATT), v_buf.reshape(bd, w, D_ATT), *rel_tabs)
        att_parts = [att.reshape(1, bd * t_new, D_ATT)]
        new_k, new_v = shp(k_a, D_ATT), shp(v_a, D_ATT)
        u_c, qk_r, v_r, g_r = shp(u_c, 2 * CONV_CH), shp(qk_r, 2 * D_RET_K), shp(v_r, D_RET_V), shp(g_r, D_RET_V)
    conv_ts = min(u_c.shape[1], 256)
    conv_out, new_conv = _conv(u_c, conv_state, conv_w, conv_b, conv_ln_g, conv_ln_b, conv_ts)
    ret_out, new_ret = _retention(qk_r, v_r, g_r, ret_state, pos)
    conv_out = conv_out.reshape(b, s, CONV_CH)
    ret_out = ret_out.reshape(b, s, D_RET_V)
    ts2 = min(ts, 256)
    x1, h2, eid, gate = _outproj(x, att_parts, conv_out, ret_out, g1, sh2, sc2, norm2_g, w_out_b, w_rt, b_rt, ts2)
    x2, y_final = _moe(x1, h2, eid, gate, g2, w_gate, w_up, w_down, final_g, bs, ts2)
    return x2, y_final, new_k, new_v, new_conv, new_ret


def kernel(x_prompt, x_sample, c_prompt, c_sample, cache_attn_k, cache_attn_v, state_conv, state_ret, rel_bias_table, norm1_g, norm2_g, w_ada, b_ada, w_in, w_out, conv_w, conv_b, conv_ln_g, conv_ln_b, moe_w_group, moe_b_group, moe_w_router, moe_b_router, moe_w_gate, moe_w_up, moe_w_down, final_norm_g):
    depth = w_ada.shape[0]
    b, s, d = x_prompt.shape
    bd, t_new, _ = x_sample.shape
    w_buf = cache_attn_k.shape[2]
    past_len = 16384
    pos_p = jnp.arange(s, dtype=jnp.int32)
    pos_s = past_len + jnp.arange(t_new, dtype=jnp.int32)

    mods_all = _ada(jnp.concatenate([c_prompt, c_sample], axis=0), w_ada, b_ada)
    prompt_tabs = [_prompt_bias(rel_bias_table, dil) for _, dil in BRANCHES]
    sample_tabs = _sample_bias(rel_bias_table, w_buf, t_new)
    w_in_b = w_in.astype(BF16)
    w_out_b = w_out.astype(BF16)
    pad_rt = LANES - N_EXPERTS - N_GROUPS
    w_rt = jnp.pad(jnp.concatenate([moe_w_router, moe_w_group], axis=-1), ((0, 0), (0, 0), (0, pad_rt)))
    b_rt = jnp.pad(jnp.concatenate([moe_b_router, moe_b_group], axis=-1), ((0, 0), (0, pad_rt)))

    xp = x_prompt
    xs = x_sample.reshape(1, bd * t_new, d)
    outs = {k: [] for k in ("pk", "pv", "pc", "pr", "sk", "sv", "sc", "sr")}
    yp = ys = None
    for l in range(depth):
        lw = (norm1_g[l], norm2_g[l], w_in_b[l], w_out_b[l], conv_w[l], conv_b[l], conv_ln_g[l], conv_ln_b[l],
              w_rt[l], b_rt[l].reshape(1, LANES), moe_w_gate[l], moe_w_up[l], moe_w_down[l])
        final_g = final_norm_g if l == depth - 1 else None
        m = mods_all[l]
        mods_p = [m[:b, i * d:(i + 1) * d].reshape(b, 1, d) for i in range(6)]
        mods_s = [jnp.repeat(m[b:, i * d:(i + 1) * d], t_new, axis=0).reshape(1, bd * t_new, d) for i in range(6)]
        xp, yp, k_n, v_n, c_n, r_n = _layer(xp, mods_p, pos_p, None, prompt_tabs, lw, final_g, ts=512, bs=256)
        outs["pk"].append(k_n.reshape(b, -1, ATT_HEADS, ATT_HD))
        outs["pv"].append(v_n.reshape(b, -1, ATT_HEADS, ATT_HD))
        outs["pc"].append(c_n)
        outs["pr"].append(r_n)
        caches = (cache_attn_k[l], cache_attn_v[l], state_conv[l], state_ret[l], bd, t_new)
        xs, ys, k_n, v_n, c_n, r_n = _layer(xs, mods_s, pos_s, caches, sample_tabs, lw, final_g,
                                            ts=bd * t_new, bs=16)
        outs["sk"].append(k_n.reshape(bd, t_new, ATT_HEADS, ATT_HD))
        outs["sv"].append(v_n.reshape(bd, t_new, ATT_HEADS, ATT_HD))
        outs["sc"].append(c_n)
        outs["sr"].append(r_n)
    st = lambda key: jnp.stack(outs[key])
    return (yp, ys.reshape(bd, t_new, d), st("pk"), st("pv"), st("pc"), st("pr"),
            st("sk"), st("sv"), st("sc"), st("sr"))
```

```python
import functools
import math

import numpy as np
import jax
import jax.numpy as jnp
from jax import lax
from jax.experimental import pallas as pl
from jax.experimental.pallas import tpu as pltpu

F32 = jnp.float32
BF16 = jnp.bfloat16

D_MODEL = 1024
ATT_HEADS = 8
ATT_HD = 64
D_ATT = ATT_HEADS * ATT_HD
BRANCHES = ((128, 1), (512, 4), (2048, 16))
STEPS = 128
REL_BUCKETS = 32
REL_MAX_DIST = 2048
CONV_CH = 256
CONV_WIDTH = 31
HALO = 32
RET_HEADS = 4
RET_HDK = 32
RET_HDV = 64
D_RET_K = RET_HEADS * RET_HDK
D_RET_V = RET_HEADS * RET_HDV
RET_CHUNK = 128
N_GROUPS = 4
EXPERTS_PER_GROUP = 8
N_EXPERTS = N_GROUPS * EXPERTS_PER_GROUP
D_EXPERT = 512
NORM_EPS = 1e-6
LN_EPS = 1e-5
NEG_INF = -1e30
LANES = 128
ROW_TILES = D_MODEL // LANES
IN_COLS = ((0, 512), (512, 1024), (1024, 1536), (1536, 2048), (2048, 2304), (2304, 2560), (2560, 2816))
D_IN = 2816
VMEM_LIMIT = 48 * 1024 * 1024

NT = (((1,), (1,)), ((), ()))
TN = (((0,), (0,)), ((), ()))


def _params(*sem):
    return pltpu.CompilerParams(dimension_semantics=sem, vmem_limit_bytes=VMEM_LIMIT)


def _dot(a, b):
    return jnp.dot(a, b, preferred_element_type=F32)


def _split_bf16(a):
    hi = a.astype(BF16)
    return hi, (a - hi.astype(F32)).astype(BF16)


def _dot3(a, b):
    ah, al = _split_bf16(a)
    bh, bl = _split_bf16(b)
    return _dot(ah, bh) + (_dot(al, bh) + _dot(ah, bl))


def _sigmoid(x):
    return 1.0 / (1.0 + jnp.exp(-x))


def _ada_kernel(c_ref, w_ref, b_ref, o_ref):
    c = c_ref[...]
    o_ref[0] = _dot3(c * _sigmoid(c), w_ref[0]) + b_ref[0]


def _ada(c_all, w_ada, b_ada):
    depth, d, n6 = w_ada.shape
    bc = c_all.shape[0]
    tn = 1536
    return pl.pallas_call(
        _ada_kernel,
        grid=(depth, n6 // tn),
        in_specs=[pl.BlockSpec((bc, d), lambda l, j: (0, 0)),
                  pl.BlockSpec((1, d, tn), lambda l, j: (l, 0, j)),
                  pl.BlockSpec((1, 1, tn), lambda l, j: (l, 0, j))],
        out_specs=pl.BlockSpec((1, bc, tn), lambda l, j: (l, 0, j)),
        out_shape=jax.ShapeDtypeStruct((depth, bc, n6), F32),
        compiler_params=_params("arbitrary", "arbitrary"),
        name="ada",
    )(c_all, w_ada, b_ada.reshape(depth, 1, n6))


def _inproj_kernel(x_ref, sh_ref, sc_ref, g_ref, w_ref, *out_refs):
    x = x_ref[0]
    ms = jnp.mean(x * x, axis=-1, keepdims=True)
    y = x * lax.rsqrt(ms + NORM_EPS) * g_ref[...]
    h = (y * (1.0 + sc_ref[0]) + sh_ref[0]).astype(BF16)
    for ref, (a, b) in zip(out_refs, IN_COLS):
        ref[0] = _dot(h, w_ref[:, a:b])


def _inproj(x, sh, sc, g, w_bf16, ts):
    b, s, d = x.shape
    mr = sh.shape[1]
    mod_map = (lambda bi, i: (bi, 0, 0)) if mr == 1 else (lambda bi, i: (bi, i, 0))
    mod_blk = (1, 1, d) if mr == 1 else (1, ts, d)
    widths = [c1 - c0 for c0, c1 in IN_COLS]
    return pl.pallas_call(
        _inproj_kernel,
        grid=(b, s // ts),
        in_specs=[pl.BlockSpec((1, ts, d), lambda bi, i: (bi, i, 0)),
                  pl.BlockSpec(mod_blk, mod_map),
                  pl.BlockSpec(mod_blk, mod_map),
                  pl.BlockSpec((1, d), lambda bi, i: (0, 0)),
                  pl.BlockSpec((d, D_IN), lambda bi, i: (0, 0))],
        out_specs=[pl.BlockSpec((1, ts, w), lambda bi, i: (bi, i, 0)) for w in widths],
        out_shape=[jax.ShapeDtypeStruct((b, s, w), F32) for w in widths],
        compiler_params=_params("parallel", "parallel"),
        name="inproj",
    )(x, sh, sc, g.reshape(1, d), w_bf16)


def _t5_bucket(dist):
    n = np.asarray(dist)
    max_exact = REL_BUCKETS // 2
    large = max_exact + (np.log(np.maximum(n, 1) / max_exact) / math.log(REL_MAX_DIST / max_exact)
                         * (REL_BUCKETS - max_exact)).astype(np.int64)
    large = np.minimum(large, REL_BUCKETS - 1)
    return np.where(n < max_exact, n, large).astype(np.int32)


def _lookup(rel_bias_table, bucket):
    onehot = np.zeros((bucket.shape[0], REL_BUCKETS), np.float32)
    onehot[np.arange(bucket.shape[0]), bucket] = 1.0
    return jnp.dot(jnp.asarray(onehot), rel_bias_table.astype(F32), precision=lax.Precision.HIGHEST)


def _prompt_bias(rel_bias_table):
    period = 3 * STEPS
    x = np.arange(period)
    col_minus_row = np.where(x < 2 * STEPS, x, x - period)
    outs = []
    for _, dil in BRANCHES:
        vecs = []
        for koff in (0, STEPS):
            delta = koff - col_minus_row
            valid = (delta >= 0) & (delta <= STEPS)
            vals = _lookup(rel_bias_table, _t5_bucket(dil * np.clip(delta, 0, STEPS)))
            vecs.append(jnp.where(valid[:, None], vals, NEG_INF))
        a = jnp.transpose(jnp.stack(vecs, axis=0), (2, 0, 1))
        t = jnp.tile(a, (1, 1, STEPS))[:, :, :STEPS * (period - 1)]
        t = t.reshape(ATT_HEADS, 2, STEPS, period - 1)[..., :2 * STEPS]
        outs.append(t.reshape(ATT_HEADS // 2, 2, 2, STEPS, 2 * STEPS))
    return outs


def _sample_bias(rel_bias_table, w, t_new):
    n = w + t_new
    delta = w + t_new - 1 - np.arange(n + t_new - 1)
    tabs = []
    for window, dil in BRANCHES:
        valid = (delta >= 0) & (delta % dil == 0) & (delta <= window)
        vals = _lookup(rel_bias_table, _t5_bucket(np.clip(delta, 0, window)))
        g = jnp.where(valid[:, None], vals, NEG_INF)
        per_t = jnp.stack([g[t_new - 1 - t:t_new - 1 - t + n] for t in range(t_new)], axis=2)
        tabs.append(per_t.reshape(n, ATT_HEADS * t_new))
    t = jnp.stack(tabs, axis=0)
    return t[:, :w], t[:, w:]


SPAN = 2048
UNITS_PER_STEP = 4


def _attn_units(q_ref, k_ref, v_ref, specs):
    lane = lax.broadcasted_iota(jnp.int32, (1, LANES), 1)
    masks = [lane < ATT_HD, lane >= ATT_HD]
    vs, ss = [], []
    for qstart, kstart, var, d, bias_ref in specs:
        qsl = pl.ds(qstart, STEPS) if d == 1 else pl.ds(qstart, STEPS, stride=d)
        ksl = pl.ds(kstart, 2 * STEPS) if d == 1 else pl.ds(kstart, 2 * STEPS, stride=d)
        q = q_ref[0, qsl, :] * (ATT_HD ** -0.5)
        k = k_ref[0, ksl, :].astype(BF16)
        vs.append(v_ref[0, ksl, :].astype(BF16))
        for hh in range(2):
            qh = jnp.where(masks[hh], q, 0.0).astype(BF16)
            ss.append(lax.dot_general(qh, k, NT, preferred_element_type=F32) + bias_ref[hh, var])
    ms = [jnp.max(s, axis=-1, keepdims=True) for s in ss]
    ps = [jnp.exp(s - m) for s, m in zip(ss, ms)]
    ls = [jnp.sum(p, axis=-1, keepdims=True) for p in ps]
    pvs = [_dot(p.astype(BF16), vs[i // 2]) for i, p in enumerate(ps)]
    outs = []
    for u in range(len(specs)):
        a, b = 2 * u, 2 * u + 1
        outs.append((jnp.where(masks[1], pvs[b], pvs[a]), jnp.where(masks[1], ms[b], ms[a]),
                     jnp.where(masks[1], ls[b], ls[a])))
    return outs


def _attn_prompt_kernel(q_ref, k_ref, v_ref, b1_ref, b4_ref, b16_ref, o_ref, acc, mm, ll, *, s):
    bias_refs = (b1_ref, b4_ref, b16_ref)
    for span in range(s // SPAN):
        base = span * SPAN
        for (_, d), bref in zip(BRANCHES, bias_refs):
            njt = SPAN // (d * STEPS)

            def step(it, carry, d=d, bref=bref, njt=njt):
                specs, locs = [], []
                for uu in range(UNITS_PER_STEP):
                    u = it * UNITS_PER_STEP + uu
                    r = u // njt
                    j0 = base // d + (u % njt) * STEPS
                    specs.append((r + d * j0, r + d * jnp.maximum(j0 - STEPS, 0), jnp.minimum(j0, 1), d, bref))
                    loc0 = r + d * (u % njt) * STEPS
                    locs.append(pl.ds(pl.multiple_of(loc0, STEPS), STEPS) if d == 1 else pl.ds(loc0, STEPS, stride=d))
                for (pv, m, l), loc in zip(_attn_units(q_ref, k_ref, v_ref, specs), locs):
                    if d == 1:
                        acc[loc, :] = pv
                        mm[loc, :] = m
                        ll[loc, :] = l
                    else:
                        m_old = mm[loc, :]
                        m_new = jnp.maximum(m_old, m)
                        a = jnp.exp(m_old - m_new)
                        b = jnp.exp(m - m_new)
                        acc[loc, :] = acc[loc, :] * a + pv * b
                        ll[loc, :] = ll[loc, :] * a + l * b
                        mm[loc, :] = m_new
                return carry

            lax.fori_loop(0, d * njt // UNITS_PER_STEP, step, 0)

        for c0 in range(0, SPAN, 2 * STEPS):
            rows = slice(c0, c0 + 2 * STEPS)
            o_ref[0, base + c0:base + c0 + 2 * STEPS, :] = acc[rows, :] * (1.0 / ll[rows, :])


def _attn_prompt(q, k, v, biases):
    b, s, _ = q.shape
    assert s % (2 * SPAN) == 0, "every strided subsequence must hold at least two 128-step tiles"
    hp = ATT_HEADS // 2
    blk = pl.BlockSpec((1, s, LANES), lambda bi, h: (bi, 0, h))
    bsp = pl.BlockSpec((None, 2, 2, STEPS, 2 * STEPS), lambda bi, h: (h, 0, 0, 0, 0))
    return pl.pallas_call(
        functools.partial(_attn_prompt_kernel, s=s),
        grid=(b, hp),
        in_specs=[blk, blk, blk, bsp, bsp, bsp],
        out_specs=blk,
        out_shape=jax.ShapeDtypeStruct((b, s, D_ATT), F32),
        scratch_shapes=[pltpu.VMEM((SPAN, LANES), F32)] * 3,
        compiler_params=_params("parallel", "parallel"),
        name="attn_prompt",
    )(q, k, v, *biases)


def _attn_sample_kernel(qbd_ref, kc_ref, vc_ref, kn_ref, vn_ref, bc_ref, bn_ref, o_ref, *, t_new):
    qbd = (qbd_ref[0] * (ATT_HD ** -0.5)).astype(BF16)
    sc = _dot(kc_ref[0].astype(BF16), qbd)
    sn = _dot(kn_ref[0].astype(BF16), qbd)
    nb = len(BRANCHES)
    m = None
    for i in range(nb):
        mi = jnp.maximum(jnp.max(sc + bc_ref[i], axis=0, keepdims=True),
                         jnp.max(sn + bn_ref[i], axis=0, keepdims=True))
        m = mi if m is None else jnp.maximum(m, mi)
    pc = jnp.exp(sc + bc_ref[0] - m)
    pn = jnp.exp(sn + bn_ref[0] - m)
    for i in range(1, nb):
        pc = pc + jnp.exp(sc + bc_ref[i] - m)
        pn = pn + jnp.exp(sn + bn_ref[i] - m)
    pc = pc.astype(BF16)
    pn = pn.astype(BF16)
    w = pc.shape[0]
    ob = (lax.dot_general(pc, vc_ref[0].astype(BF16), TN, preferred_element_type=F32)
          + lax.dot_general(pn, vn_ref[0].astype(BF16), TN, preferred_element_type=F32))
    lb = (lax.dot_general(pc, jnp.ones((w, LANES), BF16), TN, preferred_element_type=F32)
          + lax.dot_general(pn, jnp.ones((t_new, LANES), BF16), TN, preferred_element_type=F32))
    ob = ob * (1.0 / lb[:, 0:1])
    lane_head = lax.broadcasted_iota(jnp.int32, (1, D_ATT), 1) // ATT_HD
    out = jnp.zeros((t_new, D_ATT), F32)
    for h in range(ATT_HEADS):
        out = jnp.where(lane_head == h, ob[h * t_new:(h + 1) * t_new, :], out)
    o_ref[0] = out


def _attn_sample(q, k_new, v_new, k_buf, v_buf, bias_c, bias_n):
    bd, t_new, _ = q.shape
    w = k_buf.shape[1]
    assert t_new == 8, "lane layout head*T+t assumes 8 new positions"
    ht = ATT_HEADS * t_new
    qh = q.reshape(bd, t_new, ATT_HEADS, ATT_HD)
    qbd = jnp.einsum("bthc,hg->bhcgt", qh, jnp.eye(ATT_HEADS, dtype=F32)).reshape(bd, D_ATT, ht)
    return pl.pallas_call(
        functools.partial(_attn_sample_kernel, t_new=t_new),
        grid=(bd,),
        in_specs=[pl.BlockSpec((1, D_ATT, ht), lambda b: (b, 0, 0)),
                  pl.BlockSpec((1, w, D_ATT), lambda b: (b, 0, 0)),
                  pl.BlockSpec((1, w, D_ATT), lambda b: (b, 0, 0)),
                  pl.BlockSpec((1, t_new, D_ATT), lambda b: (b, 0, 0)),
                  pl.BlockSpec((1, t_new, D_ATT), lambda b: (b, 0, 0)),
                  pl.BlockSpec((len(BRANCHES), w, ht), lambda b: (0, 0, 0)),
                  pl.BlockSpec((len(BRANCHES), t_new, ht), lambda b: (0, 0, 0))],
        out_specs=pl.BlockSpec((1, t_new, D_ATT), lambda b: (b, 0, 0)),
        out_shape=jax.ShapeDtypeStruct((bd, t_new, D_ATT), F32),
        compiler_params=_params("parallel"),
        name="attn_sample",
    )(qbd, k_buf, v_buf, k_new, v_new, bias_c, bias_n)


def _conv_kernel(*refs, ts, use_halo):
    if use_halo:
        u_ref, hu_ref, st_ref, w_ref, cb_ref, lg_ref, lb_ref, o_ref, ns_ref, hc = refs
    else:
        u_ref, st_ref, w_ref, cb_ref, lg_ref, lb_ref, o_ref, ns_ref, hc = refs
    u = u_ref[0]
    h = u[:, :CONV_CH] * _sigmoid(u[:, CONV_CH:])
    halo = st_ref[0]
    if use_halo:
        hu = hu_ref[0]
        halo = jnp.where(pl.program_id(1) == 0, halo, hu[:, :CONV_CH] * _sigmoid(hu[:, CONV_CH:]))
    hc[0:HALO, :] = halo
    hc[HALO:HALO + ts, :] = h
    rows = min(ts, 64)
    off = HALO - (CONV_WIDTH - 1)
    for r0 in range(0, ts, rows):
        acc = jnp.broadcast_to(cb_ref[...], (rows, CONV_CH))
        for j in range(CONV_WIDTH):
            acc = acc + w_ref[j:j + 1, :] * hc[r0 + off + j:r0 + off + j + rows, :]
        mu = jnp.mean(acc, axis=-1, keepdims=True)
        xc = acc - mu
        var = jnp.mean(xc * xc, axis=-1, keepdims=True)
        y = xc * lax.rsqrt(var + LN_EPS) * lg_ref[...] + lb_ref[...]
        o_ref[0, r0:r0 + rows, :] = y * _sigmoid(y)
    ns_ref[0] = hc[ts:ts + HALO, :]


def _conv(u_c, state, conv_w, conv_b, ln_g, ln_b, ts):
    b, s, _ = u_c.shape
    n_t = s // ts
    use_halo = n_t > 1
    st = jnp.pad(state, ((0, 0), (HALO - (CONV_WIDTH - 1), 0), (0, 0)))
    w = jnp.pad(conv_w, ((0, HALO - CONV_WIDTH), (0, 0)))
    vec = lambda a: a.reshape(1, CONV_CH)
    in_specs = [pl.BlockSpec((1, ts, 2 * CONV_CH), lambda bi, i: (bi, i, 0))]
    args = [u_c]
    if use_halo:
        per = ts // HALO
        in_specs.append(pl.BlockSpec((1, HALO, 2 * CONV_CH), lambda bi, i: (bi, jnp.maximum(i * per - 1, 0), 0)))
        args.append(u_c)
    in_specs += [pl.BlockSpec((1, HALO, CONV_CH), lambda bi, i: (bi, 0, 0)),
                 pl.BlockSpec((HALO, CONV_CH), lambda bi, i: (0, 0))]
    in_specs += [pl.BlockSpec((1, CONV_CH), lambda bi, i: (0, 0))] * 3
    args += [st, w, vec(conv_b), vec(ln_g), vec(ln_b)]
    out, ns = pl.pallas_call(
        functools.partial(_conv_kernel, ts=ts, use_halo=use_halo),
        grid=(b, n_t),
        in_specs=in_specs,
        out_specs=[pl.BlockSpec((1, ts, CONV_CH), lambda bi, i: (bi, i, 0)),
                   pl.BlockSpec((1, HALO, CONV_CH), lambda bi, i: (bi, 0, 0))],
        out_shape=[jax.ShapeDtypeStruct((b, s, CONV_CH), F32),
                   jax.ShapeDtypeStruct((b, HALO, CONV_CH), F32)],
        scratch_shapes=[pltpu.VMEM((ts + HALO, CONV_CH), F32)],
        compiler_params=_params("parallel", "arbitrary"),
        name="conv",
    )(*args)
    return out, ns[:, HALO - (CONV_WIDTH - 1):]


def _ret_kernel(qk_ref, v_ref, g_ref, cos_ref, sin_ref, intra_ref, qd_ref, kd_ref, cd_ref, bm_ref, s0_ref,
                o_ref, so_ref, s_scr):
    @pl.when(pl.program_id(1) == 0)
    def _():
        s_scr[...] = s0_ref[0]

    qk = qk_ref[0]
    cos = cos_ref[...]
    sin = sin_ref[...]
    lane = lax.broadcasted_iota(jnp.int32, (1, D_RET_K), 1)
    first_half = (lane % RET_HDK) < (RET_HDK // 2)

    def rot(x):
        swapped = jnp.where(first_half, pltpu.roll(x, D_RET_K - RET_HDK // 2, 1), pltpu.roll(x, RET_HDK // 2, 1))
        return x * cos + swapped * sin

    q = rot(qk[:, :D_RET_K]) * (RET_HDK ** -0.5)
    k = rot(qk[:, D_RET_K:])
    kb = k.astype(BF16)
    vb = v_ref[0].astype(BF16)
    state = s_scr[...]
    lane_q = lane // RET_HDK
    lane_v = lax.broadcasted_iota(jnp.int32, (1, D_RET_V), 1) // RET_HDV
    o = _dot((q * qd_ref[...]).astype(BF16), state.astype(BF16))
    for h in range(RET_HEADS):
        qh = jnp.where(lane_q == h, q, 0.0).astype(BF16)
        scores = lax.dot_general(qh, kb, NT, preferred_element_type=F32) * intra_ref[h]
        o = o + jnp.where(lane_v == h, _dot(scores.astype(BF16), vb), 0.0)
    kv = lax.dot_general((k * kd_ref[...]).astype(BF16), vb, TN, preferred_element_type=F32)
    new_state = cd_ref[...] * state + bm_ref[...] * kv
    s_scr[...] = new_state
    so_ref[0] = new_state
    res = jnp.zeros_like(o)
    for h in range(RET_HEADS):
        mh = lane_v == h
        mu = jnp.sum(jnp.where(mh, o, 0.0), axis=-1, keepdims=True) * (1.0 / RET_HDV)
        xc = jnp.where(mh, o - mu, 0.0)
        var = jnp.sum(xc * xc, axis=-1, keepdims=True) * (1.0 / RET_HDV)
        res = res + xc * lax.rsqrt(var + LN_EPS)
    g = g_ref[0]
    o_ref[0] = g * _sigmoid(g) * res


def _retention(qk, v, g, state, pos):
    b, s, _ = qk.shape
    chunk = RET_CHUNK if s % RET_CHUNK == 0 else s
    n_c = s // chunk
    half = RET_HDK // 2
    inv = 1.0 / (10000.0 ** jnp.linspace(0.0, 1.0, half, dtype=F32))
    ang = pos.astype(F32)[:, None] * inv[None, :]
    cos = jnp.tile(jnp.cos(ang), (1, 2 * RET_HEADS))
    sin = jnp.tile(jnp.concatenate([-jnp.sin(ang), jnp.sin(ang)], axis=1), (1, RET_HEADS))
    log_g = jnp.log1p(-jnp.exp2(-5.0 - jnp.arange(RET_HEADS, dtype=F32)))
    i = jnp.arange(chunk, dtype=F32)
    dist = i[:, None] - i[None, :]
    intra = jnp.where(dist >= 0, jnp.exp(log_g[:, None, None] * jnp.maximum(dist, 0.0)), 0.0)
    q_decay = jnp.repeat(jnp.exp(log_g[None, :] * (i[:, None] + 1.0)), RET_HDK, axis=1)
    k_decay = jnp.repeat(jnp.exp(log_g[None, :] * (chunk - 1.0 - i[:, None])), RET_HDK, axis=1)
    cd = jnp.broadcast_to(jnp.repeat(jnp.exp(log_g * chunk), RET_HDK)[:, None], (D_RET_K, D_RET_V))
    eye = jnp.eye(RET_HEADS, dtype=F32)
    bm = jnp.repeat(jnp.repeat(eye, RET_HDK, axis=0), RET_HDV, axis=1)
    s0 = jnp.einsum("bhde,hg->bhdge", state.astype(F32), eye).reshape(b, D_RET_K, D_RET_V)
    seq = lambda w: pl.BlockSpec((1, chunk, w), lambda bi, c: (bi, c, 0))
    tab = lambda r, w: pl.BlockSpec((r, w), lambda bi, c: (0, 0))
    out, s_fin = pl.pallas_call(
        _ret_kernel,
        grid=(b, n_c),
        in_specs=[seq(2 * D_RET_K), seq(D_RET_V), seq(D_RET_V),
                  pl.BlockSpec((chunk, D_RET_K), lambda bi, c: (c, 0)),
                  pl.BlockSpec((chunk, D_RET_K), lambda bi, c: (c, 0)),
                  pl.BlockSpec((RET_HEADS, chunk, chunk), lambda bi, c: (0, 0, 0)),
                  tab(chunk, D_RET_K), tab(chunk, D_RET_K), tab(D_RET_K, D_RET_V), tab(D_RET_K, D_RET_V),
                  pl.BlockSpec((1, D_RET_K, D_RET_V), lambda bi, c: (bi, 0, 0))],
        out_specs=[seq(D_RET_V), pl.BlockSpec((1, D_RET_K, D_RET_V), lambda bi, c: (bi, 0, 0))],
        out_shape=[jax.ShapeDtypeStruct((b, s, D_RET_V), F32),
                   jax.ShapeDtypeStruct((b, D_RET_K, D_RET_V), F32)],
        scratch_shapes=[pltpu.VMEM((D_RET_K, D_RET_V), F32)],
        compiler_params=_params("parallel", "arbitrary"),
        name="retention",
    )(qk, v, g, cos, sin, intra, q_decay, k_decay, cd, bm, s0)
    s5 = s_fin.reshape(b, RET_HEADS, RET_HDK, RET_HEADS, RET_HDV)
    new_state = jnp.stack([s5[:, h, :, h, :] for h in range(RET_HEADS)], axis=1)
    return out, new_state


ROUTER_COL = 0
GROUP_COL = N_EXPERTS


def _route(logits):
    lane = lax.broadcasted_iota(jnp.int32, (1, LANES), 1)
    lane_f = lane.astype(F32)
    is_grp = (lane >= GROUP_COL) & (lane < GROUP_COL + N_GROUPS)
    gl = jnp.where(is_grp, logits, -jnp.inf)
    gmax = jnp.max(gl, axis=-1, keepdims=True)
    gsel = jnp.min(jnp.where(gl == gmax, lane_f, 1e9), axis=-1, keepdims=True) - float(GROUP_COL)
    gsum = jnp.sum(jnp.exp(gl - gmax), axis=-1, keepdims=True)
    grp_gate = 1.0 / gsum
    in_grp = (lane < N_EXPERTS) & ((lane // EXPERTS_PER_GROUP).astype(F32) == gsel)
    el = jnp.where(in_grp, logits, -jnp.inf)
    t1 = jnp.max(el, axis=-1, keepdims=True)
    i1 = jnp.min(jnp.where(el == t1, lane_f, 1e9), axis=-1, keepdims=True)
    el2 = jnp.where(lane_f == i1, -jnp.inf, el)
    t2 = jnp.max(el2, axis=-1, keepdims=True)
    i2 = jnp.min(jnp.where(el2 == t2, lane_f, 1e9), axis=-1, keepdims=True)
    a = jnp.exp(t2 - t1)
    g0 = grp_gate * (1.0 / (1.0 + a))
    g1 = grp_gate * (a / (1.0 + a))
    eid = jnp.where(lane == 0, i1, jnp.where(lane == 1, i2, 0.0)).astype(jnp.int32)
    gate = jnp.where(lane == 0, g0, jnp.where(lane == 1, g1, 0.0))
    return eid, gate


def _store_token_tiles(ref, val):
    n = val.shape[0]
    for j in range(ROW_TILES):
        ref[pl.ds(j, n, stride=ROW_TILES), :] = val[:, j * LANES:(j + 1) * LANES]


def _load_token_tiles(ref, n):
    return jnp.concatenate([ref[pl.ds(j, n, stride=ROW_TILES), :] for j in range(ROW_TILES)], axis=1)


def _outproj_kernel(x_ref, att_ref, cv_ref, rt_ref, g1_ref, sh2_ref, sc2_ref, n2_ref, wo_ref, wr_ref, br_ref,
                    xo_ref, h2_ref, eid_ref, gate_ref):
    mix = (_dot(att_ref[0].astype(BF16), wo_ref[0:D_ATT, :])
           + _dot(cv_ref[0].astype(BF16), wo_ref[D_ATT:D_ATT + CONV_CH, :])
           + _dot(rt_ref[0].astype(BF16), wo_ref[D_ATT + CONV_CH:, :]))
    x = x_ref[0] + g1_ref[0] * mix
    xo_ref[0] = x
    ms = jnp.mean(x * x, axis=-1, keepdims=True)
    h2 = x * lax.rsqrt(ms + NORM_EPS) * n2_ref[...] * (1.0 + sc2_ref[0]) + sh2_ref[0]
    _store_token_tiles(h2_ref, h2)
    eid, gate = _route(_dot3(h2, wr_ref[...]) + br_ref[...])
    eid_ref[0] = eid
    gate_ref[0] = gate


def _outproj(x, att, conv_out, ret_out, g1, sh2, sc2, n2g, wo_bf16, w_rt, b_rt, ts):
    b, s, d = x.shape
    n_t = s // ts
    mr = g1.shape[1]
    mod_map = (lambda bi, i: (bi, 0, 0)) if mr == 1 else (lambda bi, i: (bi, i, 0))
    mod_blk = (1, 1, d) if mr == 1 else (1, ts, d)
    seq = lambda w: pl.BlockSpec((1, ts, w), lambda bi, i: (bi, i, 0))
    const = lambda r, w: pl.BlockSpec((r, w), lambda bi, i: (0, 0))
    in_specs = ([seq(d), seq(D_ATT), seq(CONV_CH), seq(D_RET_V)] + [pl.BlockSpec(mod_blk, mod_map)] * 3
                + [const(1, d), const(d, d), const(d, LANES), const(1, LANES)])
    return pl.pallas_call(
        _outproj_kernel,
        grid=(b, n_t),
        in_specs=in_specs,
        out_specs=[seq(d), pl.BlockSpec((ts * ROW_TILES, LANES), lambda bi, i: (bi * n_t + i, 0)),
                   seq(LANES), seq(LANES)],
        out_shape=[jax.ShapeDtypeStruct((b, s, d), F32),
                   jax.ShapeDtypeStruct((b * s * ROW_TILES, LANES), F32),
                   jax.ShapeDtypeStruct((b, s, LANES), jnp.int32), jax.ShapeDtypeStruct((b, s, LANES), F32)],
        compiler_params=_params("parallel", "parallel"),
        name="outproj",
    )(x, att, conv_out, ret_out, g1, sh2, sc2, n2g.reshape(1, d), wo_bf16, w_rt, b_rt)


GATHER_ROWS = 512


def _gather_kernel(idx_ref, x_hbm, o_ref, sem, *, rows):
    def issue(r, carry):
        src = pl.multiple_of(idx_ref[0, 0, r] * ROW_TILES, ROW_TILES)
        dst = pl.multiple_of(r * ROW_TILES, ROW_TILES)
        pltpu.make_async_copy(x_hbm.at[pl.ds(src, ROW_TILES)], o_ref.at[pl.ds(dst, ROW_TILES)], sem).start()
        return carry

    lax.fori_loop(0, rows, issue, 0, unroll=8)
    pltpu.make_async_copy(o_ref, o_ref, sem).wait()


def _gather_rows(x, idx):
    n_out = idx.shape[0]
    rows = max(r for r in range(8, GATHER_ROWS + 1, 8) if n_out % r == 0)
    steps = n_out // rows
    return pl.pallas_call(
        functools.partial(_gather_kernel, rows=rows),
        grid=(steps,),
        in_specs=[pl.BlockSpec((1, 1, rows), lambda i: (i, 0, 0), memory_space=pltpu.SMEM),
                  pl.BlockSpec(memory_space=pl.ANY)],
        out_specs=pl.BlockSpec((rows * ROW_TILES, LANES), lambda i: (i, 0)),
        out_shape=jax.ShapeDtypeStruct((n_out * ROW_TILES, LANES), x.dtype),
        scratch_shapes=[pltpu.SemaphoreType.DMA(())],
        compiler_params=_params("arbitrary"),
        name="gather_rows",
    )(idx.reshape(steps, 1, rows), x)


def _ffn_kernel(be_ref, nu_ref, x_ref, wg_ref, wu_ref, wd_ref, y_ref, *, bs):
    i = pl.program_id(0)

    @pl.when(i < nu_ref[0])
    def _():
        x = _load_token_tiles(x_ref, bs).astype(BF16)
        g = _dot(x, wg_ref[0].astype(BF16))
        u = _dot(x, wu_ref[0].astype(BF16))
        h = (g * _sigmoid(g) * u).astype(BF16)
        _store_token_tiles(y_ref, _dot(h, wd_ref[0].astype(BF16)))

    @pl.when(i >= nu_ref[0])
    def _():
        y_ref[...] = jnp.zeros_like(y_ref)


def _ffn(x_buf, blk_expert, n_used, w_gate, w_up, w_down, bs):
    d = D_MODEL
    n_blk = x_buf.shape[0] // (bs * ROW_TILES)
    tok_blk = pl.BlockSpec((bs * ROW_TILES, LANES), lambda i, be, nu: (i, 0))
    grid_spec = pltpu.PrefetchScalarGridSpec(
        num_scalar_prefetch=2,
        grid=(n_blk,),
        in_specs=[tok_blk,
                  pl.BlockSpec((1, d, D_EXPERT), lambda i, be, nu: (be[i], 0, 0)),
                  pl.BlockSpec((1, d, D_EXPERT), lambda i, be, nu: (be[i], 0, 0)),
                  pl.BlockSpec((1, D_EXPERT, d), lambda i, be, nu: (be[i], 0, 0))],
        out_specs=tok_blk,
    )
    return pl.pallas_call(
        functools.partial(_ffn_kernel, bs=bs),
        grid_spec=grid_spec,
        out_shape=jax.ShapeDtypeStruct(x_buf.shape, F32),
        compiler_params=_params("arbitrary"),
        name="moe_ffn",
    )(blk_expert, n_used, x_buf, w_gate, w_up, w_down)


def _combine_kernel(*refs, final, ts):
    if final:
        x_ref, y0_ref, y1_ref, gate_ref, g2_ref, fg_ref, xo_ref, yn_ref = refs
    else:
        x_ref, y0_ref, y1_ref, gate_ref, g2_ref, xo_ref = refs
    gate = gate_ref[0]
    y = _load_token_tiles(y0_ref, ts) * gate[:, 0:1] + _load_token_tiles(y1_ref, ts) * gate[:, 1:2]
    x = x_ref[0] + g2_ref[0] * y
    xo_ref[0] = x
    if final:
        ms = jnp.mean(x * x, axis=-1, keepdims=True)
        yn_ref[0] = x * lax.rsqrt(ms + NORM_EPS) * fg_ref[...]


def _combine(x, yg, gate, g2, final_g, ts):
    b, s, d = x.shape
    n_t = s // ts
    slot1 = b * n_t
    mr = g2.shape[1]
    mod_map = (lambda bi, i: (bi, 0, 0)) if mr == 1 else (lambda bi, i: (bi, i, 0))
    mod_blk = (1, 1, d) if mr == 1 else (1, ts, d)
    seq = lambda w: pl.BlockSpec((1, ts, w), lambda bi, i: (bi, i, 0))
    final = final_g is not None
    in_specs = [seq(d),
                pl.BlockSpec((ts * ROW_TILES, LANES), lambda bi, i: (bi * n_t + i, 0)),
                pl.BlockSpec((ts * ROW_TILES, LANES), lambda bi, i: (slot1 + bi * n_t + i, 0)),
                seq(LANES), pl.BlockSpec(mod_blk, mod_map)]
    args = [x, yg, yg, gate, g2]
    out_specs = [seq(d)]
    out_shape = [jax.ShapeDtypeStruct((b, s, d), F32)]
    if final:
        in_specs.append(pl.BlockSpec((1, d), lambda bi, i: (0, 0)))
        args.append(final_g.reshape(1, d))
        out_specs.append(seq(d))
        out_shape.append(jax.ShapeDtypeStruct((b, s, d), F32))
    res = pl.pallas_call(
        functools.partial(_combine_kernel, final=final, ts=ts),
        grid=(b, n_t),
        in_specs=in_specs,
        out_specs=out_specs,
        out_shape=out_shape,
        compiler_params=_params("parallel", "parallel"),
        name="moe_combine",
    )(*args)
    return (res[0], res[1]) if final else (res[0], None)


def _dispatch(expert, bs):
    n_tok = expert.shape[0]
    n_asg = 2 * n_tok
    e_flat = expert.reshape(-1)
    ids = jnp.arange(N_EXPERTS, dtype=jnp.int32)
    onehot = e_flat[:, None] == ids[None, :]
    csum = jnp.cumsum(onehot.astype(jnp.int32), axis=0)
    counts = csum[-1]
    rank = jnp.sum(jnp.where(onehot, csum, 0), axis=1) - 1
    start = jnp.cumsum(counts) - counts
    padded = (counts + bs - 1) // bs * bs
    pend = jnp.cumsum(padded)
    pstart = pend - padded
    dest = jnp.sum(jnp.where(onehot, pstart[None, :], 0), axis=1) + rank
    order = jnp.argsort(e_flat, stable=True).astype(jnp.int32)
    n_blk = -(-n_asg // bs) + N_EXPERTS
    row = jnp.arange(n_blk * bs, dtype=jnp.int32)
    in_e = (row[:, None] >= pstart[None, :]) & (row[:, None] < (pstart + counts)[None, :])
    sorted_idx = jnp.sum(jnp.where(in_e, (start - pstart)[None, :] + row[:, None], 0), axis=1)
    src_tok = jnp.where(jnp.any(in_e, axis=1), order[sorted_idx] // 2, 0).astype(jnp.int32)
    blk_start = jnp.arange(n_blk, dtype=jnp.int32) * bs
    blk_expert = jnp.minimum(jnp.sum((blk_start[:, None] >= pend[None, :]).astype(jnp.int32), axis=1), N_EXPERTS - 1)
    n_used = (pend[-1] // bs).astype(jnp.int32).reshape(1)
    return src_tok, dest.astype(jnp.int32), blk_expert.astype(jnp.int32), n_used


def _moe(x, h2_tiles, eid, gate, g2, w_gate, w_up, w_down, final_g, bs, ts):
    b, s, d = x.shape
    n_tok = b * s
    expert = eid.reshape(n_tok, LANES)[:, :2]
    src_tok, dest, blk_expert, n_used = _dispatch(expert, bs)
    x_buf = _gather_rows(h2_tiles, src_tok)
    y_buf = _ffn(x_buf, blk_expert, n_used, w_gate, w_up, w_down, bs)
    dest2 = dest.reshape(n_tok, 2).T.reshape(-1)
    yg = _gather_rows(y_buf, dest2)
    return _combine(x, yg, gate, g2, final_g, ts)


def _layer(x, mods, pos, caches, rel_tabs, lw, final_g, ts, bs):
    (norm1_g, norm2_g, w_in_b, w_out_b, conv_w, conv_b, conv_ln_g, conv_ln_b, w_rt, b_rt,
     w_gate, w_up, w_down) = lw
    sh1, sc1, g1, sh2, sc2, g2 = mods
    b, s, d = x.shape
    q_a, k_a, v_a, u_c, qk_r, v_r, g_r = _inproj(x, sh1, sc1, norm1_g, w_in_b, ts)
    if caches is None:
        att = _attn_prompt(q_a, k_a, v_a, rel_tabs)
        keep = min(BRANCHES[-1][0], s)
        new_k, new_v = k_a[:, s - keep:], v_a[:, s - keep:]
        conv_state = jnp.zeros((b, CONV_WIDTH - 1, CONV_CH), F32)
        ret_state = jnp.zeros((b, RET_HEADS, RET_HDK, RET_HDV), F32)
    else:
        k_buf, v_buf, conv_state, ret_state, bd, t_new = caches
        w = k_buf.shape[1]
        shp = lambda t, c: t.reshape(bd, t_new, c)
        att = _attn_sample(shp(q_a, D_ATT), shp(k_a, D_ATT), shp(v_a, D_ATT),
                           k_buf.reshape(bd, w, D_ATT), v_buf.reshape(bd, w, D_ATT), *rel_tabs)
        att = att.reshape(1, bd * t_new, D_ATT)
        new_k, new_v = shp(k_a, D_ATT), shp(v_a, D_ATT)
        u_c, qk_r, v_r, g_r = shp(u_c, 2 * CONV_CH), shp(qk_r, 2 * D_RET_K), shp(v_r, D_RET_V), shp(g_r, D_RET_V)
    conv_ts = min(u_c.shape[1], 256)
    conv_out, new_conv = _conv(u_c, conv_state, conv_w, conv_b, conv_ln_g, conv_ln_b, conv_ts)
    ret_out, new_ret = _retention(qk_r, v_r, g_r, ret_state, pos)
    conv_out = conv_out.reshape(b, s, CONV_CH)
    ret_out = ret_out.reshape(b, s, D_RET_V)
    ts2 = min(ts, 256)
    x1, h2, eid, gate = _outproj(x, att, conv_out, ret_out, g1, sh2, sc2, norm2_g, w_out_b, w_rt, b_rt, ts2)
    x2, y_final = _moe(x1, h2, eid, gate, g2, w_gate, w_up, w_down, final_g, bs, ts2)
    return x2, y_final, new_k, new_v, new_conv, new_ret


def kernel(x_prompt, x_sample, c_prompt, c_sample, cache_attn_k, cache_attn_v, state_conv, state_ret, rel_bias_table, norm1_g, norm2_g, w_ada, b_ada, w_in, w_out, conv_w, conv_b, conv_ln_g, conv_ln_b, moe_w_group, moe_b_group, moe_w_router, moe_b_router, moe_w_gate, moe_w_up, moe_w_down, final_norm_g):
    depth = w_ada.shape[0]
    b, s, d = x_prompt.shape
    bd, t_new, _ = x_sample.shape
    w_buf = cache_attn_k.shape[2]
    past_len = 16384
    pos_p = jnp.arange(s, dtype=jnp.int32)
    pos_s = past_len + jnp.arange(t_new, dtype=jnp.int32)

    mods_all = _ada(jnp.concatenate([c_prompt, c_sample], axis=0), w_ada, b_ada)
    prompt_tabs = _prompt_bias(rel_bias_table)
    sample_tabs = _sample_bias(rel_bias_table, w_buf, t_new)
    w_in_b = w_in.astype(BF16)
    w_out_b = w_out.astype(BF16)
    pad_rt = LANES - N_EXPERTS - N_GROUPS
    w_rt = jnp.pad(jnp.concatenate([moe_w_router, moe_w_group], axis=-1), ((0, 0), (0, 0), (0, pad_rt)))
    b_rt = jnp.pad(jnp.concatenate([moe_b_router, moe_b_group], axis=-1), ((0, 0), (0, pad_rt)))

    xp = x_prompt
    xs = x_sample.reshape(1, bd * t_new, d)
    outs = {k: [] for k in ("pk", "pv", "pc", "pr", "sk", "sv", "sc", "sr")}
    yp = ys = None
    for l in range(depth):
        lw = (norm1_g[l], norm2_g[l], w_in_b[l], w_out_b[l], conv_w[l], conv_b[l], conv_ln_g[l], conv_ln_b[l],
              w_rt[l], b_rt[l].reshape(1, LANES), moe_w_gate[l], moe_w_up[l], moe_w_down[l])
        final_g = final_norm_g if l == depth - 1 else None
        m = mods_all[l]
        mods_p = [m[:b, i * d:(i + 1) * d].reshape(b, 1, d) for i in range(6)]
        mods_s = [jnp.repeat(m[b:, i * d:(i + 1) * d], t_new, axis=0).reshape(1, bd * t_new, d) for i in range(6)]
        xp, yp, k_n, v_n, c_n, r_n = _layer(xp, mods_p, pos_p, None, prompt_tabs, lw, final_g, ts=512, bs=256)
        outs["pk"].append(k_n.reshape(b, -1, ATT_HEADS, ATT_HD))
        outs["pv"].append(v_n.reshape(b, -1, ATT_HEADS, ATT_HD))
        outs["pc"].append(c_n)
        outs["pr"].append(r_n)
        caches = (cache_attn_k[l], cache_attn_v[l], state_conv[l], state_ret[l], bd, t_new)
        xs, ys, k_n, v_n, c_n, r_n = _layer(xs, mods_s, pos_s, caches, sample_tabs, lw, final_g,
                                            ts=bd * t_new, bs=16)
        outs["sk"].append(k_n.reshape(bd, t_new, ATT_HEADS, ATT_HD))
        outs["sv"].append(v_n.reshape(bd, t_new, ATT_HEADS, ATT_HD))
        outs["sc"].append(c_n)
        outs["sr"].append(r_n)
    st = lambda key: jnp.stack(outs[key])
    return (yp, ys.reshape(bd, t_new, d), st("pk"), st("pv"), st("pc"), st("pr"),
            st("sk"), st("sv"), st("sc"), st("sr"))
```
